```python
import math
import jax, jax.numpy as jnp
from jax import lax
import numpy as np

D_MODEL = 2048
BATCH = 2
SEQ = 4096
DEPTH = 4
DEC_BATCH = 8
DEC_SEQ = 1
PAST_LEN = 16384
PAGE_SIZE = 128

D_MIX = D_MODEL
D_LRU = D_MIX // 2
LRU_BLOCKS = 8
LRU_BW = D_LRU // LRU_BLOCKS
CONV_W = 4
LRU_C = 8.0
N_HEADS = 8
HEAD_DIM = (D_MIX - D_LRU) // N_HEADS
D_ATT = N_HEADS * HEAD_DIM
IDX_HEADS = 16
IDX_DIM = 64
D_QIDX = IDX_HEADS * IDX_DIM
TOPK_MAX = 256
ROPE_THETA = 10000.0
D_FF = ((-(-(8 * D_MODEL) // 3) + 255) // 256) * 256
ALPHA = (2 * DEPTH) ** 0.25
BETA = (8 * DEPTH) ** -0.25
QBLK = 64
LN_EPS = 1e-5
SPLIT_POINTS = (D_LRU, 2 * D_LRU, 2 * D_LRU + D_ATT, 2 * D_LRU + 2 * D_ATT,
                2 * D_LRU + 3 * D_ATT, 2 * D_LRU + 3 * D_ATT + D_QIDX,
                2 * D_LRU + 3 * D_ATT + D_QIDX + IDX_DIM)
D_IN = SPLIT_POINTS[-1] + IDX_HEADS

kernel_name = "hymba_rglru_dsa_deepnorm_step"


def layer_norm(x, g, b):
    xf = x.astype(jnp.float32)
    mu = jnp.mean(xf, -1, keepdims=True)
    var = jnp.mean(jnp.square(xf - mu), -1, keepdims=True)
    y = (xf - mu) * lax.rsqrt(var + LN_EPS) * g.astype(jnp.float32) + b.astype(jnp.float32)
    return y.astype(x.dtype)


def rope(x, pos):
    half = x.shape[-1] // 2
    inv = ROPE_THETA ** (-jnp.arange(half, dtype=jnp.float32) / half)
    ang = pos.astype(jnp.float32)[:, None] * inv[None, :]
    cos = jnp.cos(ang)[:, None, :]
    sin = jnp.sin(ang)[:, None, :]
    xf = x.astype(jnp.float32)
    x1, x2 = xf[..., :half], xf[..., half:]
    return jnp.concatenate([x1 * cos - x2 * sin, x2 * cos + x1 * sin], -1).astype(x.dtype)


def split_projection(x, w_in_l, pos):
    B, T, _ = x.shape
    p = jnp.einsum('btd,de->bte', x, w_in_l)
    xb, gate, q, k, v, qi, ki, wi = jnp.split(p, SPLIT_POINTS, axis=-1)
    q = rope(q.reshape(B, T, N_HEADS, HEAD_DIM), pos)
    k = rope(k.reshape(B, T, N_HEADS, HEAD_DIM), pos)
    v = v.reshape(B, T, N_HEADS, HEAD_DIM)
    qi = rope(qi.reshape(B, T, IDX_HEADS, IDX_DIM), pos)
    ki = rope(ki[:, :, None, :], pos)[:, :, 0, :]
    wi = wi * (IDX_HEADS * IDX_DIM) ** -0.5
    return xb, gate, q, k, v, qi, ki, wi


def rglru(xb, gate, conv_buf, h0, conv_w, conv_b, ga_w, ga_b, gx_w, gx_b, lam):
    B, T, _ = xb.shape
    xp = jnp.concatenate([conv_buf.astype(xb.dtype), xb], axis=1)
    xc = conv_b + sum(xp[:, j:j + T] * conv_w[j] for j in range(CONV_W))
    new_buf = xp[:, T:]
    xblk = xc.reshape(B, T, LRU_BLOCKS, LRU_BW)
    r = jax.nn.sigmoid(jnp.einsum('btnd,nde->btne', xblk, ga_w).reshape(B, T, D_LRU) + ga_b)
    i = jax.nn.sigmoid(jnp.einsum('btnd,nde->btne', xblk, gx_w).reshape(B, T, D_LRU) + gx_b)
    log_a = -LRU_C * jax.nn.softplus(-lam.astype(jnp.float32)) * r.astype(jnp.float32)
    a = jnp.exp(log_a)
    bterm = jnp.sqrt(-jnp.expm1(2.0 * log_a)) * (i.astype(jnp.float32) * xc.astype(jnp.float32))
    bterm = bterm.at[:, 0].add(a[:, 0] * h0.astype(jnp.float32))

    def combine(c1, c2):
        a1, b1 = c1
        a2, b2 = c2
        return a1 * a2, a2 * b1 + b2

    _, h = lax.associative_scan(combine, (a, bterm), axis=1)
    y = h.astype(xb.dtype) * jax.nn.gelu(gate)
    return y, h[:, -1], new_buf


def index_scores(qi, wi, ki, q_pos, k_pos):
    s = jax.nn.relu(jnp.einsum('bqhd,bsd->bqhs', qi.astype(jnp.float32), ki.astype(jnp.float32)))
    score = jnp.einsum('bqh,bqhs->bqs', wi.astype(jnp.float32), s)
    return jnp.where(k_pos[None, None, :] <= q_pos[None, :, None], score, -jnp.inf)


def sparse_softmax(q, kg, vg, valid):
    logits = jnp.einsum('bqhd,bqkhd->bqhk', q.astype(jnp.float32), kg.astype(jnp.float32)) * HEAD_DIM ** -0.5
    logits = jnp.where(valid[:, :, None, :], logits, -jnp.inf)
    p = jax.nn.softmax(logits, axis=-1)
    return jnp.einsum('bqhk,bqkhd->bqhd', p, vg.astype(jnp.float32)).astype(q.dtype)


def gather_rows(a, idx):
    return jax.vmap(lambda ab, ib: ab[ib])(a, idx)


def prompt_attention(q, k, v, qi, wi, ki):
    B, T = q.shape[:2]
    k_sel = min(TOPK_MAX, T // 4)
    nb = T // QBLK
    pos = jnp.arange(T, dtype=jnp.int32)

    def to_blocks(a):
        return jnp.swapaxes(a.reshape((B, nb, QBLK) + a.shape[2:]), 0, 1)

    def block(xs):
        qb, qib, wib, posb = xs
        score = index_scores(qib, wib, ki, posb, pos)
        _, idx = lax.top_k(score, k_sel)
        valid = idx <= posb[None, :, None]
        return sparse_softmax(qb, gather_rows(k, idx), gather_rows(v, idx), valid)

    out = lax.map(block, (to_blocks(q), to_blocks(qi), to_blocks(wi), pos.reshape(nb, QBLK)))
    return jnp.swapaxes(out, 0, 1).reshape(B, T, D_ATT)


def sample_attention(q, k_new, v_new, qi, wi, ki_new, cache_k, cache_v, cache_kidx, page_table, l):
    Bd, T = q.shape[:2]
    past = page_table.shape[1] * PAGE_SIZE
    L = past + T
    k_sel = min(TOPK_MAX, L // 4)
    ki_past = cache_kidx[l, page_table].reshape(Bd, past, IDX_DIM)
    ki_all = jnp.concatenate([ki_past.astype(ki_new.dtype), ki_new], axis=1)
    q_pos = past + jnp.arange(T, dtype=jnp.int32)
    k_pos = jnp.arange(L, dtype=jnp.int32)
    score = index_scores(qi, wi, ki_all, q_pos, k_pos)
    _, idx = lax.top_k(score, k_sel)
    valid = idx <= q_pos[None, :, None]
    pidx = jnp.minimum(idx, past - 1)
    phys = jax.vmap(lambda pt, j: pt[j])(page_table, pidx // PAGE_SIZE)
    off = pidx % PAGE_SIZE
    kp = cache_k[l, phys, off]
    vp = cache_v[l, phys, off]
    nidx = jnp.clip(idx - past, 0, T - 1)
    is_new = (idx >= past)[..., None, None]
    kg = jnp.where(is_new, gather_rows(k_new, nidx), kp.astype(k_new.dtype))
    vg = jnp.where(is_new, gather_rows(v_new, nidx), vp.astype(v_new.dtype))
    return sparse_softmax(q, kg, vg, valid).reshape(Bd, T, D_ATT)


def swiglu(x, wg, wu, wd):
    return (jax.nn.silu(x @ wg) * (x @ wu)) @ wd


def finish_layer(x, lru_y, att_y, w_out_l, ln1_g_l, ln1_b_l, wg, wu, wd, ln2_g_l, ln2_b_l):
    mix = jnp.concatenate([lru_y, att_y], axis=-1) @ w_out_l
    x = layer_norm(ALPHA * x + mix, ln1_g_l, ln1_b_l)
    return layer_norm(ALPHA * x + swiglu(x, wg, wu, wd), ln2_g_l, ln2_b_l)


def setup_inputs(seed: int = 0) -> dict:
    key = jax.random.key(seed)
    ks = jax.random.split(key, 26)
    n_pages = PAST_LEN // PAGE_SIZE
    n_pool = (5 * DEC_BATCH * n_pages) // 4
    f32 = jnp.float32
    nrm = lambda k, shape, s=1.0: jax.random.normal(k, shape, f32) * s
    u = jax.random.uniform(ks[0], (DEPTH, D_LRU), f32, 0.9, 0.999)
    p_a = u ** (1.0 / LRU_C)
    lru_lambda = jnp.log(p_a) - jnp.log1p(-p_a)
    page_table = jax.random.permutation(ks[1], n_pool)[:DEC_BATCH * n_pages].reshape(DEC_BATCH, n_pages).astype(jnp.int32)
    return {
        'x_prompt': nrm(ks[2], (BATCH, SEQ, D_MODEL)),
        'x_sample': nrm(ks[3], (DEC_BATCH, DEC_SEQ, D_MODEL)),
        'cache_k': nrm(ks[4], (DEPTH, n_pool, PAGE_SIZE, N_HEADS, HEAD_DIM)),
        'cache_v': nrm(ks[5], (DEPTH, n_pool, PAGE_SIZE, N_HEADS, HEAD_DIM)),
        'cache_kidx': nrm(ks[6], (DEPTH, n_pool, PAGE_SIZE, IDX_DIM)),
        'state_lru_h': nrm(ks[7], (DEPTH, DEC_BATCH, D_LRU), 0.5),
        'state_lru_conv': nrm(ks[8], (DEPTH, DEC_BATCH, CONV_W - 1, D_LRU)),
        'page_table': page_table,
        'w_in': nrm(ks[9], (DEPTH, D_MODEL, D_IN), D_MODEL ** -0.5),
        'conv_w': nrm(ks[10], (DEPTH, CONV_W, D_LRU), CONV_W ** -0.5),
        'conv_b': nrm(ks[11], (DEPTH, D_LRU), 0.01),
        'gate_a_w': nrm(ks[12], (DEPTH, LRU_BLOCKS, LRU_BW, LRU_BW), LRU_BW ** -0.5),
        'gate_a_b': nrm(ks[13], (DEPTH, D_LRU), 0.01),
        'gate_x_w': nrm(ks[14], (DEPTH, LRU_BLOCKS, LRU_BW, LRU_BW), LRU_BW ** -0.5),
        'gate_x_b': nrm(ks[15], (DEPTH, D_LRU), 0.01),
        'lru_lambda': lru_lambda,
        'w_out': nrm(ks[16], (DEPTH, D_MIX, D_MODEL), BETA * D_MIX ** -0.5),
        'ln1_g': 1.0 + nrm(ks[17], (DEPTH, D_MODEL), 0.01),
        'ln1_b': nrm(ks[18], (DEPTH, D_MODEL), 0.01),
        'w_ffn_gate': nrm(ks[19], (DEPTH, D_MODEL, D_FF), D_MODEL ** -0.5),
        'w_ffn_up': nrm(ks[20], (DEPTH, D_MODEL, D_FF), D_MODEL ** -0.5),
        'w_ffn_down': nrm(ks[21], (DEPTH, D_FF, D_MODEL), BETA * D_FF ** -0.5),
        'ln2_g': 1.0 + nrm(ks[22], (DEPTH, D_MODEL), 0.01),
        'ln2_b': nrm(ks[23], (DEPTH, D_MODEL), 0.01),
    }


def reference(x_prompt, x_sample, cache_k, cache_v, cache_kidx, state_lru_h, state_lru_conv, page_table,
              w_in, conv_w, conv_b, gate_a_w, gate_a_b, gate_x_w, gate_x_b, lru_lambda, w_out,
              ln1_g, ln1_b, w_ffn_gate, w_ffn_up, w_ffn_down, ln2_g, ln2_b):
    Bp, Tp, _ = x_prompt.shape
    Ts = x_sample.shape[1]
    past = page_table.shape[1] * PAGE_SIZE
    pos_p = jnp.arange(Tp, dtype=jnp.int32)
    pos_s = past + jnp.arange(Ts, dtype=jnp.int32)
    xp, xs = x_prompt, x_sample
    kp_l, vp_l, kip_l, hp_l, cp_l = [], [], [], [], []
    ks_l, vs_l, kis_l, hs_l, cs_l = [], [], [], [], []
    for l in range(DEPTH):
        lru_params = (conv_w[l], conv_b[l], gate_a_w[l], gate_a_b[l], gate_x_w[l], gate_x_b[l], lru_lambda[l])
        post = (w_out[l], ln1_g[l], ln1_b[l], w_ffn_gate[l], w_ffn_up[l], w_ffn_down[l], ln2_g[l], ln2_b[l])
        xb, gt, q, k, v, qi, ki, wi = split_projection(xp, w_in[l], pos_p)
        lru_y, h_last, conv_last = rglru(xb, gt, jnp.zeros((Bp, CONV_W - 1, D_LRU), xp.dtype),
                                         jnp.zeros((Bp, D_LRU), jnp.float32), *lru_params)
        att_y = prompt_attention(q, k, v, qi, wi, ki)
        xp = finish_layer(xp, lru_y, att_y, *post)
        kp_l.append(k); vp_l.append(v); kip_l.append(ki); hp_l.append(h_last); cp_l.append(conv_last)
        xb, gt, q, k, v, qi, ki, wi = split_projection(xs, w_in[l], pos_s)
        lru_y, h_last, conv_last = rglru(xb, gt, state_lru_conv[l], state_lru_h[l], *lru_params)
        att_y = sample_attention(q, k, v, qi, wi, ki, cache_k, cache_v, cache_kidx, page_table, l)
        xs = finish_layer(xs, lru_y, att_y, *post)
        ks_l.append(k); vs_l.append(v); kis_l.append(ki); hs_l.append(h_last); cs_l.append(conv_last)
    return (xp, xs,
            jnp.stack(kp_l), jnp.stack(vp_l), jnp.stack(kip_l), jnp.stack(hp_l), jnp.stack(cp_l),
            jnp.stack(ks_l), jnp.stack(vs_l), jnp.stack(kis_l), jnp.stack(hs_l), jnp.stack(cs_l))
```

```python
import functools

import jax
import jax.numpy as jnp
from jax import lax
from jax.experimental import pallas as pl
from jax.experimental.pallas import tpu as pltpu

F32 = jnp.float32
BF16 = jnp.bfloat16
I32 = jnp.int32

D_MODEL = 2048
D_LRU = 1024
LRU_BLOCKS = 8
LRU_BW = 128
CONV_W = 4
LRU_C = 8.0
N_HEADS = 8
HEAD_DIM = 128
D_ATT = N_HEADS * HEAD_DIM
IDX_HEADS = 16
IDX_DIM = 64
D_QIDX = IDX_HEADS * IDX_DIM
TOPK = 256
ROPE_THETA = 10000.0
PAGE = 128
LN_EPS = 1e-5
KIWI_W = 128
WI_SCALE = float(D_QIDX) ** -0.5
Q_SCALE = float(HEAD_DIM) ** -0.5

INT_MIN = -(2 ** 31)
NEG_BIG = -1e30
SAMPLE_ROWS = 16
VMEM_LIMIT = 56 * 1024 * 1024


def _cparams(semantics):
    return pltpu.CompilerParams(dimension_semantics=semantics, vmem_limit_bytes=VMEM_LIMIT)


def _dot(a, b):
    return jnp.dot(a, b, preferred_element_type=F32)


def _dot_nt(a, b):
    return lax.dot_general(a, b, (((1,), (1,)), ((), ())), preferred_element_type=F32)


def _sigmoid(x):
    return 1.0 / (1.0 + jnp.exp(-x))


def _rope(x, cos, sin, half):
    n = x.shape[-1]
    lane = lax.broadcasted_iota(I32, x.shape, 1)
    first = (lane & (2 * half - 1)) < half
    partner = jnp.where(first, pltpu.roll(x, n - half, 1), pltpu.roll(x, half, 1))
    reps = n // cos.shape[-1]
    if reps > 1:
        cos = jnp.tile(cos, (1, reps))
        sin = jnp.tile(sin, (1, reps))
    return x * cos + partner * sin


def _rope_tables(pos, head_dim):
    half = head_dim // 2
    inv = ROPE_THETA ** (-jnp.arange(half, dtype=F32) / half)
    ang = pos.astype(F32)[:, None] * inv[None, :]
    cos = jnp.cos(ang)
    sin = jnp.sin(ang)
    cos = jnp.concatenate([cos, cos], -1)
    sin = jnp.concatenate([-sin, sin], -1)
    reps = 128 // head_dim
    return jnp.tile(cos, (1, reps)), jnp.tile(sin, (1, reps))


def _mm_kernel(x_ref, w_ref, o_ref):
    o_ref[...] = _dot(x_ref[...], w_ref[...]).astype(o_ref.dtype)


def _matmul(x, w, tm, tn, out_dtype):
    m, k = x.shape
    n = w.shape[1]
    return pl.pallas_call(
        _mm_kernel,
        grid=(m // tm, n // tn),
        in_specs=[pl.BlockSpec((tm, k), lambda i, j: (i, 0)),
                  pl.BlockSpec((k, tn), lambda i, j: (0, j))],
        out_specs=pl.BlockSpec((tm, tn), lambda i, j: (i, j)),
        out_shape=jax.ShapeDtypeStruct((m, n), out_dtype),
        compiler_params=_cparams(("parallel", "arbitrary")),
        name="proj_plain",
    )(x, w)


def _qkv_kernel(x_ref, w_ref, cos_ref, sin_ref, q_ref, kf_ref, kb_ref, vf_ref, vb_ref):
    j = pl.program_id(1)
    p = _dot(x_ref[...], w_ref[...])

    @pl.when(j == 0)
    def _():
        q = _rope(p, cos_ref[...], sin_ref[...], HEAD_DIM // 2)
        q_ref[...] = (q * Q_SCALE).astype(BF16)

    @pl.when(j == 1)
    def _():
        k = _rope(p, cos_ref[...], sin_ref[...], HEAD_DIM // 2)
        kf_ref[...] = k
        kb_ref[...] = k.astype(BF16)

    @pl.when(j == 2)
    def _():
        vf_ref[...] = p
        vb_ref[...] = p.astype(BF16)


def _proj_qkv(x, w, cos, sin, tm):
    m, k = x.shape
    nt = cos.shape[0] // tm
    row = lambda i, j: (i, 0)
    out_blk = pl.BlockSpec((tm, D_ATT), row)
    sds = lambda dt: jax.ShapeDtypeStruct((m, D_ATT), dt)
    return pl.pallas_call(
        _qkv_kernel,
        grid=(m // tm, 3),
        in_specs=[pl.BlockSpec((tm, k), row),
                  pl.BlockSpec((k, D_ATT), lambda i, j: (0, j)),
                  pl.BlockSpec((tm, 128), lambda i, j: (i % nt, 0)),
                  pl.BlockSpec((tm, 128), lambda i, j: (i % nt, 0))],
        out_specs=[out_blk] * 5,
        out_shape=[sds(BF16), sds(F32), sds(BF16), sds(F32), sds(BF16)],
        compiler_params=_cparams(("parallel", "arbitrary")),
        name="proj_qkv",
    )(x, w, cos, sin)


def _qi_kernel(x_ref, w_ref, cos_ref, sin_ref, o_ref):
    p = _dot(x_ref[...], w_ref[...])
    o_ref[...] = _rope(p, cos_ref[...], sin_ref[...], IDX_DIM // 2).astype(BF16)


def _proj_qi(x, w, cos, sin, tm, tn):
    m, k = x.shape
    n = w.shape[1]
    nt = cos.shape[0] // tm
    return pl.pallas_call(
        _qi_kernel,
        grid=(m // tm, n // tn),
        in_specs=[pl.BlockSpec((tm, k), lambda i, j: (i, 0)),
                  pl.BlockSpec((k, tn), lambda i, j: (0, j)),
                  pl.BlockSpec((tm, 128), lambda i, j: (i % nt, 0)),
                  pl.BlockSpec((tm, 128), lambda i, j: (i % nt, 0))],
        out_specs=pl.BlockSpec((tm, tn), lambda i, j: (i, j)),
        out_shape=jax.ShapeDtypeStruct((m, n), BF16),
        compiler_params=_cparams(("parallel", "arbitrary")),
        name="proj_qi",
    )(x, w, cos, sin)


def _kiwi_kernel(x_ref, w_ref, cos_ref, sin_ref, of_ref, ob_ref):
    p = _dot(x_ref[...], w_ref[...])
    r = _rope(p, cos_ref[...], sin_ref[...], IDX_DIM // 2)
    lane = lax.broadcasted_iota(I32, p.shape, 1)
    of_ref[...] = jnp.where(lane < IDX_DIM, r,
                            jnp.where(lane < IDX_DIM + IDX_HEADS, p * WI_SCALE, 0.0))
    ob_ref[...] = jnp.where(lane < IDX_DIM, r, pltpu.roll(r, IDX_DIM, 1)).astype(BF16)


def _proj_kiwi(x, w, cos, sin, tm):
    m, k = x.shape
    nt = cos.shape[0] // tm
    blk = pl.BlockSpec((tm, KIWI_W), lambda i: (i, 0))
    return pl.pallas_call(
        _kiwi_kernel,
        grid=(m // tm,),
        in_specs=[pl.BlockSpec((tm, k), lambda i: (i, 0)),
                  pl.BlockSpec((k, KIWI_W), lambda i: (0, 0)),
                  pl.BlockSpec((tm, 128), lambda i: (i % nt, 0)),
                  pl.BlockSpec((tm, 128), lambda i: (i % nt, 0))],
        out_specs=[blk, blk],
        out_shape=[jax.ShapeDtypeStruct((m, KIWI_W), F32), jax.ShapeDtypeStruct((m, KIWI_W), BF16)],
        compiler_params=_cparams(("parallel",)),
        name="proj_kiwi",
    )(x, w, cos, sin)


def _gelu_tanh(x):
    return 0.5 * x * (1.0 + jnp.tanh(0.7978845608028654 * (x + 0.044715 * (x * x * x))))


def _lru_coeffs(xc, ga, gx, gab, gxb, lam):
    r = _sigmoid(ga + gab)
    i = _sigmoid(gx + gxb)
    z = -lam
    softplus = jnp.maximum(z, 0.0) + jnp.log(1.0 + jnp.exp(-jnp.abs(z)))
    log_a = (-LRU_C * softplus) * r
    a = jnp.exp(log_a)
    b = jnp.sqrt(1.0 - a * a) * (i * xc)
    return a, b


def _lru_prompt_kernel(xb_ref, gt_ref, cw_ref, cb_ref, gw_ref, gab_ref, gxb_ref, lam_ref,
                       y_ref, hl_ref, cl_ref, xext_ref, h_ref, *, tc):
    t = pl.program_id(2)

    @pl.when(t == 0)
    def _():
        xext_ref[0:8, :] = jnp.zeros((8, LRU_BW), F32)
        h_ref[...] = jnp.zeros_like(h_ref)

    x = xb_ref[...]
    xext_ref[8:8 + tc, :] = x
    cw = cw_ref[...]
    xc = (cb_ref[...] + cw[3:4] * x + cw[2:3] * xext_ref[7:7 + tc, :]
          + cw[1:2] * xext_ref[6:6 + tc, :] + cw[0:1] * xext_ref[5:5 + tc, :])
    cl_ref[0] = xext_ref[tc + 5:tc + 8, :]
    xext_ref[0:8, :] = x[tc - 8:tc, :]

    g = _dot(xc.astype(BF16), gw_ref[0])
    a, b = _lru_coeffs(xc, g[:, :LRU_BW], g[:, LRU_BW:], gab_ref[...], gxb_ref[...], lam_ref[...])

    row = lax.broadcasted_iota(I32, (tc, LRU_BW), 0)
    d = 1
    while d < tc:
        a_sh = pltpu.roll(a, d, 0)
        b_sh = pltpu.roll(b, d, 0)
        live = row >= d
        b = jnp.where(live, a * b_sh + b, b)
        a = jnp.where(live, a * a_sh, a)
        d *= 2
    h = a * h_ref[...] + b
    h_ref[...] = h[tc - 1:tc, :]
    hl_ref[0] = h[tc - 1:tc, :]
    y_ref[...] = (h * _gelu_tanh(gt_ref[...])).astype(BF16)


def _lru_prompt(xg, conv_w, conv_b, gw, gab, gxb, lam, batch, seq, tc):
    nt = seq // tc
    vec = pl.BlockSpec((1, LRU_BW), lambda b, n, t: (0, n))
    y, h_last, conv_last = pl.pallas_call(
        functools.partial(_lru_prompt_kernel, tc=tc),
        grid=(batch, LRU_BLOCKS, nt),
        in_specs=[pl.BlockSpec((tc, LRU_BW), lambda b, n, t: (b * nt + t, n)),
                  pl.BlockSpec((tc, LRU_BW), lambda b, n, t: (b * nt + t, LRU_BLOCKS + n)),
                  pl.BlockSpec((CONV_W, LRU_BW), lambda b, n, t: (0, n)),
                  vec,
                  pl.BlockSpec((1, LRU_BW, 2 * LRU_BW), lambda b, n, t: (n, 0, 0)),
                  vec, vec, vec],
        out_specs=[pl.BlockSpec((tc, LRU_BW), lambda b, n, t: (b * nt + t, n)),
                   pl.BlockSpec((1, 1, LRU_BW), lambda b, n, t: (b, 0, n)),
                   pl.BlockSpec((1, CONV_W - 1, LRU_BW), lambda b, n, t: (b, 0, n))],
        out_shape=[jax.ShapeDtypeStruct((batch * seq, D_LRU), BF16),
                   jax.ShapeDtypeStruct((batch, 1, D_LRU), F32),
                   jax.ShapeDtypeStruct((batch, CONV_W - 1, D_LRU), F32)],
        scratch_shapes=[pltpu.VMEM((tc + 8, LRU_BW), F32), pltpu.VMEM((1, LRU_BW), F32)],
        compiler_params=_cparams(("parallel", "parallel", "arbitrary")),
        name="rglru_prompt",
    )(xg, xg, conv_w, conv_b, gw, gab, gxb, lam)
    return y, h_last[:, 0], conv_last


def _lru_sample_kernel(xg_ref, c0_ref, c1_ref, c2_ref, h0_ref, cw_ref, cb_ref, gw_ref,
                       gab_ref, gxb_ref, lam_ref, y_ref, h_ref):
    xb = xg_ref[:, :D_LRU]
    gate = xg_ref[:, D_LRU:]
    cw = cw_ref[...]
    xc = (cb_ref[...] + cw[0:1] * c0_ref[...] + cw[1:2] * c1_ref[...]
          + cw[2:3] * c2_ref[...] + cw[3:4] * xb)
    xcb = xc.astype(BF16)
    ga, gx = [], []
    for n in range(LRU_BLOCKS):
        g = _dot(xcb[:, n * LRU_BW:(n + 1) * LRU_BW], gw_ref[n])
        ga.append(g[:, :LRU_BW])
        gx.append(g[:, LRU_BW:])
    a, b = _lru_coeffs(xc, jnp.concatenate(ga, -1), jnp.concatenate(gx, -1),
                       gab_ref[...], gxb_ref[...], lam_ref[...])
    h = a * h0_ref[...] + b
    h_ref[...] = h
    y_ref[...] = (h * _gelu_tanh(gate)).astype(BF16)


def _lru_sample(xg, conv_state, h0, conv_w, conv_b, gw, gab, gxb, lam):
    rows = xg.shape[0]
    return pl.pallas_call(
        _lru_sample_kernel,
        out_shape=[jax.ShapeDtypeStruct((rows, D_LRU), BF16), jax.ShapeDtypeStruct((rows, D_LRU), F32)],
        compiler_params=pltpu.CompilerParams(vmem_limit_bytes=VMEM_LIMIT),
        name="rglru_sample",
    )(xg, conv_state[:, 0], conv_state[:, 1], conv_state[:, 2], h0, conv_w, conv_b, gw, gab, gxb, lam)


def _sortable_key(s):
    s = jnp.where(s == 0.0, 0.0, s)
    bits = pltpu.bitcast(s, I32)
    return bits ^ ((bits >> 31) & 0x7FFFFFFF)


def _attn_prompt_kernel(q_ref, qi_ref, wt_ref, k_ref, vt_ref, kd_ref, o_ref,
                        keys_ref, bias_ref, qm_ref, *, tq):
    i = pl.program_id(1)
    nk = i + 1
    ck = tq

    lane = lax.broadcasted_iota(I32, (tq, 128), 1)
    for p in range(IDX_HEADS // 2):
        slab = qi_ref[:, p * 128:(p + 1) * 128]
        zero = jnp.zeros_like(slab)
        qm_ref[2 * p] = jnp.where(lane < IDX_DIM, slab, zero)
        qm_ref[2 * p + 1] = jnp.where(lane >= IDX_DIM, slab, zero)

    qpos = i * tq + lax.broadcasted_iota(I32, (1, tq), 1)
    kidx = lax.broadcasted_iota(I32, (ck, tq), 0)

    def score_body(c, carry):
        off = pl.multiple_of(c * ck, ck)
        kd = kd_ref[0, pl.ds(off, ck), :]
        acc = jnp.zeros((ck, tq), F32)
        for h in range(IDX_HEADS):
            x = _dot_nt(kd, qm_ref[h])
            acc = acc + wt_ref[0, h:h + 1, :] * jnp.maximum(x, 0.0)
        key = _sortable_key(acc)
        keys_ref[pl.ds(off, ck), :] = jnp.where(kidx + off <= qpos, key, INT_MIN)
        return carry

    lax.fori_loop(0, nk, score_body, 0)

    kk = jnp.minimum(qpos + 1, TOPK)

    def count(pred):
        def body(c, acc):
            off = pl.multiple_of(c * ck, ck)
            m = jnp.where(pred(keys_ref[pl.ds(off, ck), :], off), 1, 0)
            return acc + m.reshape(ck // 8, 8, tq).sum(axis=0)
        acc = lax.fori_loop(0, nk, body, jnp.zeros((8, tq), I32))
        return acc.sum(axis=0, keepdims=True)

    n_nonneg = count(lambda kc, off: kc >= 0)
    t0 = jnp.where(n_nonneg >= kk, 0, INT_MIN)

    def bit_body(it, t):
        t_try = t | jnp.left_shift(1, 30 - it)
        n = count(lambda kc, off: kc >= t_try)
        return jnp.where(n >= kk, t_try, t)

    t = lax.fori_loop(0, 31, bit_body, t0)

    n_gt = count(lambda kc, off: kc > t)
    n_eq = count(lambda kc, off: kc == t)
    need = kk - n_gt
    tie = jnp.max(n_eq - need) > 0
    jbits = 13

    def j_body(it, jcut):
        j_try = jcut + jnp.left_shift(1, jbits - 1 - it)
        n = count(lambda kc, off: (kc == t) & (kidx + off < j_try))
        return jnp.where(n <= need, j_try, jcut)

    jcut = lax.fori_loop(0, jnp.where(tie, jbits, 0), j_body,
                         jnp.where(tie, 0, 2 ** jbits) + jnp.zeros((1, tq), I32))

    def bias_body(c, carry):
        off = pl.multiple_of(c * ck, ck)
        kc = keys_ref[pl.ds(off, ck), :]
        sel = (kc > t) | ((kc == t) & (kidx + off < jcut))
        bias_ref[pl.ds(off, ck), :] = jnp.where(sel, 0.0, NEG_BIG)
        return carry

    lax.fori_loop(0, nk, bias_body, 0)

    for h in range(N_HEADS):
        qh = q_ref[:, h * HEAD_DIM:(h + 1) * HEAD_DIM]

        def att_body(c, carry, h=h, qh=qh):
            m, l, acc = carry
            off = pl.multiple_of(c * ck, ck)
            kh = k_ref[0, pl.ds(off, ck), h * HEAD_DIM:(h + 1) * HEAD_DIM]
            s = _dot_nt(kh, qh) + bias_ref[pl.ds(off, ck), :]
            m_new = jnp.maximum(m, s.max(axis=0, keepdims=True))
            alpha = jnp.exp(m - m_new)
            p = jnp.exp(s - m_new)
            l = l * alpha + p.sum(axis=0, keepdims=True)
            vh = vt_ref[0, h * HEAD_DIM:(h + 1) * HEAD_DIM, pl.ds(off, ck)]
            acc = acc * alpha + _dot(vh, p.astype(BF16))
            return m_new, l, acc

        m0 = jnp.full((1, tq), NEG_BIG, F32)
        l0 = jnp.zeros((1, tq), F32)
        a0 = jnp.zeros((HEAD_DIM, tq), F32)
        _, l, acc = lax.fori_loop(0, nk, att_body, (m0, l0, a0))
        o_ref[:, h * HEAD_DIM:(h + 1) * HEAD_DIM] = (acc / l).T.astype(BF16)


def _attn_prompt(q, qi, wt, k, vt, kd, batch, seq, tq):
    nq = seq // tq
    rows = lambda b, i: (b * nq + i, 0)
    whole = lambda b, i: (b, 0, 0)
    return pl.pallas_call(
        functools.partial(_attn_prompt_kernel, tq=tq),
        grid=(batch, nq),
        in_specs=[pl.BlockSpec((tq, D_ATT), rows),
                  pl.BlockSpec((tq, D_QIDX), rows),
                  pl.BlockSpec((1, IDX_HEADS, tq), lambda b, i: (b, 0, i)),
                  pl.BlockSpec((1, seq, D_ATT), whole),
                  pl.BlockSpec((1, D_ATT, seq), whole),
                  pl.BlockSpec((1, seq, KIWI_W), whole)],
        out_specs=pl.BlockSpec((tq, D_ATT), rows),
        out_shape=jax.ShapeDtypeStruct((batch * seq, D_ATT), BF16),
        scratch_shapes=[pltpu.VMEM((seq, tq), I32), pltpu.VMEM((seq, tq), F32),
                        pltpu.VMEM((IDX_HEADS, tq, 128), BF16)],
        compiler_params=_cparams(("parallel", "arbitrary")),
        name="dsa_prompt",
    )(q, qi, wt, k, vt, kd)


def _select_sample_kernel(pt_ref, qi_ref, w_ref, kn_ref, cache_ref, idx_ref,
                          buf_ref, sem, sc_ref, *, layer, n_pages):
    b = pl.program_id(0)
    past = n_pages * PAGE

    def page_copy(p):
        return pltpu.make_async_copy(cache_ref.at[layer, pt_ref[b, p]], buf_ref.at[p], sem)

    def start(p, c):
        page_copy(p).start()
        return c

    lax.fori_loop(0, n_pages, start, 0)

    def wait(p, c):
        page_copy(p).wait()
        return c

    lax.fori_loop(0, n_pages, wait, 0)

    qi = qi_ref[0]
    w = w_ref[0]
    group = 8

    def score_body(g, c):
        kc = buf_ref[pl.ds(g * group, group)].reshape(group * PAGE, IDX_DIM).astype(BF16)
        x = _dot_nt(qi, kc)
        s = (w * jnp.maximum(x, 0.0)).sum(axis=0, keepdims=True)
        for j in range(group):
            sc_ref[pl.ds(g * group + j, 1), :] = s[:, j * PAGE:(j + 1) * PAGE]
        return c

    lax.fori_loop(0, n_pages // group, score_body, 0)

    kn = kn_ref[0].astype(BF16).astype(F32)
    x_new = (qi.astype(F32) * kn).sum(axis=1, keepdims=True)
    s_new = (w * jnp.maximum(x_new, 0.0)).sum(axis=0, keepdims=True)

    keys = _sortable_key(sc_ref[...])
    key_new = _sortable_key(s_new)
    pos = (lax.broadcasted_iota(I32, (n_pages, PAGE), 0) * PAGE
           + lax.broadcasted_iota(I32, (n_pages, PAGE), 1))

    def count(m_past, m_new):
        return jnp.sum(jnp.where(m_past, 1, 0)) + jnp.sum(jnp.where(m_new, 1, 0))

    t0 = jnp.where(count(keys >= 0, key_new >= 0) >= TOPK, 0, INT_MIN)

    def bit_body(it, t):
        t_try = t | jnp.left_shift(1, 30 - it)
        return jnp.where(count(keys >= t_try, key_new >= t_try) >= TOPK, t_try, t)

    t = lax.fori_loop(0, 31, bit_body, t0)
    need = TOPK - count(keys > t, key_new > t)
    tie = count(keys == t, key_new == t) > need
    jbits = 15

    def j_body(it, jcut):
        j_try = jcut + jnp.left_shift(1, jbits - 1 - it)
        n = count((keys == t) & (pos < j_try), (key_new == t) & (past < j_try))
        return jnp.where(n <= need, j_try, jcut)

    jcut = lax.fori_loop(0, jnp.where(tie, jbits, 0), j_body, jnp.where(tie, 0, 2 ** jbits))
    sel = (keys > t) | ((keys == t) & (pos < jcut))
    n_past = jnp.sum(jnp.where(sel, 1, 0))

    selb = jnp.where(sel, 1.0, 0.0).astype(BF16)
    r_i = lax.broadcasted_iota(I32, (PAGE, PAGE), 0)
    c_i = lax.broadcasted_iota(I32, (PAGE, PAGE), 1)
    incl = _dot(selb, jnp.where(r_i <= c_i, 1.0, 0.0).astype(BF16))
    cnt = jnp.broadcast_to(incl[:, PAGE - 1:PAGE], (n_pages, TOPK))
    pr_i = lax.broadcasted_iota(I32, (n_pages, n_pages), 0)
    pc_i = lax.broadcasted_iota(I32, (n_pages, n_pages), 1)
    pinc = _dot(jnp.where(pc_i <= pr_i, 1.0, 0.0).astype(BF16), cnt.astype(BF16))
    slot = lax.broadcasted_iota(I32, (n_pages, TOPK), 1).astype(F32)
    before = pinc <= slot
    page = jnp.sum(jnp.where(before, 1, 0), axis=0, keepdims=True)
    pexc = jnp.sum(jnp.where(before, cnt, 0.0), axis=0, keepdims=True)
    rank = slot[0:1, :] - pexc
    onehot = jnp.where(lax.broadcasted_iota(I32, (n_pages, TOPK), 0) == page, 1.0, 0.0).astype(BF16)
    incl_sel = _dot(incl.T.astype(BF16), onehot)
    off = jnp.sum(jnp.where(incl_sel <= rank, 1, 0), axis=0, keepdims=True)
    slot_i = lax.broadcasted_iota(I32, (1, TOPK), 1)
    idx_ref[0] = jnp.where(slot_i < n_past, page * PAGE + off, past)


def _select_sample(page_table, qi, w, ki_new, cache_kidx, layer):
    bd, n_pages = page_table.shape
    grid_spec = pltpu.PrefetchScalarGridSpec(
        num_scalar_prefetch=1,
        grid=(bd,),
        in_specs=[pl.BlockSpec((1, IDX_HEADS, IDX_DIM), lambda b, pt: (b, 0, 0)),
                  pl.BlockSpec((1, IDX_HEADS, 1), lambda b, pt: (b, 0, 0)),
                  pl.BlockSpec((1, 1, IDX_DIM), lambda b, pt: (b, 0, 0)),
                  pl.BlockSpec(memory_space=pl.ANY)],
        out_specs=pl.BlockSpec((1, 1, TOPK), lambda b, pt: (b, 0, 0)),
        scratch_shapes=[pltpu.VMEM((n_pages, PAGE, IDX_DIM), F32),
                        pltpu.SemaphoreType.DMA(()),
                        pltpu.VMEM((n_pages, PAGE), F32)],
    )
    return pl.pallas_call(
        functools.partial(_select_sample_kernel, layer=layer, n_pages=n_pages),
        grid_spec=grid_spec,
        out_shape=jax.ShapeDtypeStruct((bd, 1, TOPK), I32),
        compiler_params=_cparams(("arbitrary",)),
        name="dsa_sample_select",
    )(page_table, qi, w, ki_new, cache_kidx)


def _attn_sample_kernel(pt_ref, idx_ref, q_ref, kn_ref, vn_ref, ck_ref, cv_ref, o_ref,
                        kbuf_ref, vbuf_ref, sems, *, layer, n_pages):
    b = pl.program_id(0)
    past = n_pages * PAGE

    def row_copies(r):
        s = jnp.minimum(idx_ref[b, r], past - 1)
        page = pt_ref[b, s // PAGE]
        off = s % PAGE
        return (pltpu.make_async_copy(ck_ref.at[layer, page, off], kbuf_ref.at[r], sems.at[0]),
                pltpu.make_async_copy(cv_ref.at[layer, page, off], vbuf_ref.at[r], sems.at[1]))

    def start(r, c):
        ck, cv = row_copies(r)
        ck.start()
        cv.start()
        return c

    lax.fori_loop(0, TOPK, start, 0)

    def wait(r, c):
        ck, cv = row_copies(r)
        ck.wait()
        cv.wait()
        return c

    lax.fori_loop(0, TOPK, wait, 0)

    @pl.when(idx_ref[b, TOPK - 1] == past)
    def _():
        kbuf_ref[TOPK - 1] = kn_ref[0]
        vbuf_ref[TOPK - 1] = vn_ref[0]

    q = q_ref[...]
    logits = (kbuf_ref[...] * q).sum(axis=-1, keepdims=True)
    m = logits.max(axis=0, keepdims=True)
    p = jnp.exp(logits - m)
    den = p.sum(axis=0, keepdims=True)
    o_ref[...] = (p * vbuf_ref[...]).sum(axis=0, keepdims=True) / den


def _attn_sample(page_table, idx, q, k_new, v_new, cache_k, cache_v, layer):
    bd, n_pages = page_table.shape
    row = pl.BlockSpec((1, N_HEADS, HEAD_DIM), lambda b, pt, ix: (b, 0, 0))
    grid_spec = pltpu.PrefetchScalarGridSpec(
        num_scalar_prefetch=2,
        grid=(bd,),
        in_specs=[row, row, row, pl.BlockSpec(memory_space=pl.ANY), pl.BlockSpec(memory_space=pl.ANY)],
        out_specs=row,
        scratch_shapes=[pltpu.VMEM((TOPK, N_HEADS, HEAD_DIM), F32),
                        pltpu.VMEM((TOPK, N_HEADS, HEAD_DIM), F32),
                        pltpu.SemaphoreType.DMA((2,))],
    )
    return pl.pallas_call(
        functools.partial(_attn_sample_kernel, layer=layer, n_pages=n_pages),
        grid_spec=grid_spec,
        out_shape=jax.ShapeDtypeStruct((bd, N_HEADS, HEAD_DIM), F32),
        compiler_params=_cparams(("arbitrary",)),
        name="dsa_sample_attend",
    )(page_table, idx, q, k_new, v_new, cache_k, cache_v)


def _layer_norm(z, g, b):
    mu = jnp.mean(z, axis=-1, keepdims=True)
    zc = z - mu
    var = jnp.mean(zc * zc, axis=-1, keepdims=True)
    return zc * lax.rsqrt(var + LN_EPS) * g + b


def _mix_ln_kernel(yl_ref, ya_ref, w_ref, x_ref, g_ref, b_ref, of_ref, ob_ref, *, alpha):
    mix = _dot(yl_ref[...], w_ref[:D_LRU, :]) + _dot(ya_ref[...], w_ref[D_LRU:, :])
    y = _layer_norm(alpha * x_ref[...] + mix, g_ref[...], b_ref[...])
    of_ref[...] = y
    ob_ref[...] = y.astype(BF16)


def _mix_ln(y_lru, y_att, w_out, x, g, b, tm, alpha):
    m = x.shape[0]
    row = lambda i: (i, 0)
    fixed = lambda i: (0, 0)
    return pl.pallas_call(
        functools.partial(_mix_ln_kernel, alpha=alpha),
        grid=(m // tm,),
        in_specs=[pl.BlockSpec((tm, D_LRU), row), pl.BlockSpec((tm, D_ATT), row),
                  pl.BlockSpec((D_LRU + D_ATT, D_MODEL), fixed),
                  pl.BlockSpec((tm, D_MODEL), row),
                  pl.BlockSpec((1, D_MODEL), fixed), pl.BlockSpec((1, D_MODEL), fixed)],
        out_specs=[pl.BlockSpec((tm, D_MODEL), row), pl.BlockSpec((tm, D_MODEL), row)],
        out_shape=[jax.ShapeDtypeStruct((m, D_MODEL), F32), jax.ShapeDtypeStruct((m, D_MODEL), BF16)],
        compiler_params=_cparams(("parallel",)),
        name="mix_ln1",
    )(y_lru, y_att, w_out, x, g, b)


def _ffn_up_kernel(x_ref, wg_ref, wu_ref, o_ref):
    x = x_ref[...]
    g = _dot(x, wg_ref[...])
    u = _dot(x, wu_ref[...])
    o_ref[...] = (g * _sigmoid(g) * u).astype(BF16)


def _ffn_up(x, wg, wu, tm, tn):
    m, k = x.shape
    n = wg.shape[1]
    return pl.pallas_call(
        _ffn_up_kernel,
        grid=(m // tm, n // tn),
        in_specs=[pl.BlockSpec((tm, k), lambda i, j: (i, 0)),
                  pl.BlockSpec((k, tn), lambda i, j: (0, j)),
                  pl.BlockSpec((k, tn), lambda i, j: (0, j))],
        out_specs=pl.BlockSpec((tm, tn), lambda i, j: (i, j)),
        out_shape=jax.ShapeDtypeStruct((m, n), BF16),
        compiler_params=_cparams(("parallel", "arbitrary")),
        name="ffn_up",
    )(x, wg, wu)


def _ffn_down_ln_kernel(h_ref, w_ref, x_ref, g_ref, b_ref, of_ref, ob_ref, acc_ref, *, alpha):
    k = pl.program_id(1)
    part = _dot(h_ref[...], w_ref[...])

    @pl.when(k == 0)
    def _():
        acc_ref[...] = part

    @pl.when(k > 0)
    def _():
        acc_ref[...] += part

    @pl.when(k == pl.num_programs(1) - 1)
    def _():
        y = _layer_norm(alpha * x_ref[...] + acc_ref[...], g_ref[...], b_ref[...])
        of_ref[...] = y
        ob_ref[...] = y.astype(BF16)


def _ffn_down_ln(h, wd, x, g, b, tm, tk, alpha):
    m = x.shape[0]
    kdim = h.shape[1]
    row = lambda i, k: (i, 0)
    fixed = lambda i, k: (0, 0)
    return pl.pallas_call(
        functools.partial(_ffn_down_ln_kernel, alpha=alpha),
        grid=(m // tm, kdim // tk),
        in_specs=[pl.BlockSpec((tm, tk), lambda i, k: (i, k)),
                  pl.BlockSpec((tk, D_MODEL), lambda i, k: (k, 0)),
                  pl.BlockSpec((tm, D_MODEL), row),
                  pl.BlockSpec((1, D_MODEL), fixed), pl.BlockSpec((1, D_MODEL), fixed)],
        out_specs=[pl.BlockSpec((tm, D_MODEL), row), pl.BlockSpec((tm, D_MODEL), row)],
        out_shape=[jax.ShapeDtypeStruct((m, D_MODEL), F32), jax.ShapeDtypeStruct((m, D_MODEL), BF16)],
        scratch_shapes=[pltpu.VMEM((tm, D_MODEL), F32)],
        compiler_params=_cparams(("parallel", "arbitrary")),
        name="ffn_down_ln2",
    )(h, wd, x, g, b)


def _tiles(rows):
    if rows <= SAMPLE_ROWS:
        return dict(plain=rows, qkv=rows, qi=rows, kiwi=rows, mix=rows, up=rows, down=rows)
    return dict(plain=1024, qkv=512, qi=1024, kiwi=1024, mix=512, up=1024, down=512)


def _project(xb16, w, cos128, sin128, cos64, sin64):
    t = _tiles(xb16.shape[0])
    xg = _matmul(xb16, w["lru"], t["plain"], 1024, F32)
    q, kf, kb, vf, vb = _proj_qkv(xb16, w["qkv"], cos128, sin128, t["qkv"])
    qi = _proj_qi(xb16, w["qi"], cos64, sin64, t["qi"], 1024)
    kiwi_f, kiwi_b = _proj_kiwi(xb16, w["kiwi"], cos64, sin64, t["kiwi"])
    return xg, q, kf, kb, vf, vb, qi, kiwi_f, kiwi_b


def _finish(x, y_lru, y_att, w, alpha):
    t = _tiles(x.shape[0])
    x1, x1b = _mix_ln(y_lru, y_att, w["out"], x, w["ln1_g"], w["ln1_b"], t["mix"], alpha)
    h = _ffn_up(x1b, w["ffn_gate"], w["ffn_up"], t["up"], 512)
    return _ffn_down_ln(h, w["ffn_down"], x1, w["ln2_g"], w["ln2_b"], t["down"], 512, alpha)


def kernel(x_prompt, x_sample, cache_k, cache_v, cache_kidx, state_lru_h, state_lru_conv, page_table,
           w_in, conv_w, conv_b, gate_a_w, gate_a_b, gate_x_w, gate_x_b, lru_lambda, w_out,
           ln1_g, ln1_b, w_ffn_gate, w_ffn_up, w_ffn_down, ln2_g, ln2_b):
    depth = w_in.shape[0]
    bp, tp, _ = x_prompt.shape
    bd = x_sample.shape[0]
    n_pages = page_table.shape[1]
    past = n_pages * PAGE
    alpha = (2.0 * depth) ** 0.25
    pad = SAMPLE_ROWS - bd

    pos_p = jnp.arange(tp, dtype=I32)
    pos_s = jnp.full((SAMPLE_ROWS,), past, I32)
    tabs_p = _rope_tables(pos_p, HEAD_DIM) + _rope_tables(pos_p, IDX_DIM)
    tabs_s = _rope_tables(pos_s, HEAD_DIM) + _rope_tables(pos_s, IDX_DIM)

    xp = x_prompt.reshape(bp * tp, D_MODEL)
    xs = jnp.pad(x_sample.reshape(bd, D_MODEL), ((0, pad), (0, 0)))
    xp16 = xp.astype(BF16)
    xs16 = xs.astype(BF16)

    o_lru = 2 * D_LRU
    o_qkv = o_lru + 3 * D_ATT
    o_qi = o_qkv + D_QIDX
    d_in = w_in.shape[2]

    outs = {n: [] for n in ("kp", "vp", "kip", "hp", "cp", "ks", "vs", "kis", "hs", "cs")}
    for l in range(depth):
        w_l = w_in[l]
        w = dict(
            lru=w_l[:, :o_lru].astype(BF16),
            qkv=w_l[:, o_lru:o_qkv].astype(BF16),
            qi=w_l[:, o_qkv:o_qi].astype(BF16),
            kiwi=jnp.pad(w_l[:, o_qi:], ((0, 0), (0, KIWI_W - (d_in - o_qi)))).astype(BF16),
            out=w_out[l].astype(BF16),
            ffn_gate=w_ffn_gate[l].astype(BF16),
            ffn_up=w_ffn_up[l].astype(BF16),
            ffn_down=w_ffn_down[l].astype(BF16),
            ln1_g=ln1_g[l][None], ln1_b=ln1_b[l][None], ln2_g=ln2_g[l][None], ln2_b=ln2_b[l][None],
        )
        gw = jnp.concatenate([gate_a_w[l], gate_x_w[l]], axis=-1).astype(BF16)
        lru_vecs = (conv_w[l], conv_b[l][None], gw, gate_a_b[l][None], gate_x_b[l][None], lru_lambda[l][None])

        xg, q, kf, kb, vf, vb, qi, kiwi_f, kiwi_b = _project(xp16, w, *tabs_p)
        y_lru, h_last, conv_last = _lru_prompt(xg, *lru_vecs, batch=bp, seq=tp, tc=512)
        wt = jnp.swapaxes(kiwi_f.reshape(bp, tp, KIWI_W)[:, :, IDX_DIM:IDX_DIM + IDX_HEADS], 1, 2)
        vt = jnp.swapaxes(vb.reshape(bp, tp, D_ATT), 1, 2)
        y_att = _attn_prompt(q, qi, wt, kb.reshape(bp, tp, D_ATT), vt,
                             kiwi_b.reshape(bp, tp, KIWI_W), batch=bp, seq=tp, tq=256)
        xp, xp16 = _finish(xp, y_lru, y_att, w, alpha)
        outs["kp"].append(kf.reshape(bp, tp, N_HEADS, HEAD_DIM))
        outs["vp"].append(vf.reshape(bp, tp, N_HEADS, HEAD_DIM))
        outs["kip"].append(kiwi_f[:, :IDX_DIM].reshape(bp, tp, IDX_DIM))
        outs["hp"].append(h_last)
        outs["cp"].append(conv_last)

        xg, q, kf, kb, vf, vb, qi, kiwi_f, kiwi_b = _project(xs16, w, *tabs_s)
        conv_state = jnp.pad(state_lru_conv[l], ((0, pad), (0, 0), (0, 0)))
        h0 = jnp.pad(state_lru_h[l], ((0, pad), (0, 0)))
        y_lru, h_new = _lru_sample(xg, conv_state, h0, *lru_vecs)
        ki_new = kiwi_f[:bd, :IDX_DIM]
        idx = _select_sample(page_table, qi[:bd].reshape(bd, IDX_HEADS, IDX_DIM),
                             kiwi_f[:bd, IDX_DIM:IDX_DIM + IDX_HEADS, None], ki_new[:, None, :],
                             cache_kidx, l)
        k_new = kf[:bd].reshape(bd, N_HEADS, HEAD_DIM)
        v_new = vf[:bd].reshape(bd, N_HEADS, HEAD_DIM)
        att = _attn_sample(page_table, idx[:, 0], q[:bd].astype(F32).reshape(bd, N_HEADS, HEAD_DIM),
                           k_new, v_new, cache_k, cache_v, l)
        y_att = jnp.pad(att.reshape(bd, D_ATT), ((0, pad), (0, 0))).astype(BF16)
        xs, xs16 = _finish(xs, y_lru, y_att, w, alpha)
        outs["ks"].append(k_new[:, None])
        outs["vs"].append(v_new[:, None])
        outs["kis"].append(ki_new[:, None])
        outs["hs"].append(h_new[:bd])
        outs["cs"].append(jnp.concatenate([state_lru_conv[l][:, 1:], xg[:bd, None, :D_LRU]], axis=1))

    stack = lambda n: jnp.stack(outs[n])
    return (xp.reshape(bp, tp, D_MODEL), xs[:bd].reshape(bd, 1, D_MODEL),
            stack("kp"), stack("vp"), stack("kip"), stack("hp"), stack("cp"),
            stack("ks"), stack("vs"), stack("kis"), stack("hs"), stack("cs"))
```

```python
import functools
import math

import jax
import jax.numpy as jnp
from jax import lax
from jax.experimental import pallas as pl
from jax.experimental.pallas import tpu as pltpu

F32 = jnp.float32
BF16 = jnp.bfloat16
I32 = jnp.int32

D_MODEL = 2048
D_LRU = 1024
LRU_BLOCKS = 8
LRU_BW = 128
CONV_W = 4
LRU_C = 8.0
N_HEADS = 8
HEAD_DIM = 128
D_ATT = N_HEADS * HEAD_DIM
IDX_HEADS = 16
IDX_DIM = 64
D_QIDX = IDX_HEADS * IDX_DIM
D_FF = 5632
TOPK = 256
ROPE_THETA = 10000.0
PAGE = 128
LN_EPS = 1e-5
KIWI_W = 128
WI_SCALE = float(D_QIDX) ** -0.5
Q_SCALE = float(HEAD_DIM) ** -0.5 * math.log2(math.e)

O_LRU = 0
O_QKV = 2 * D_LRU
O_QI = O_QKV + 3 * D_ATT
O_KIWI = O_QI + D_QIDX

INT_MIN = -(2 ** 31)
NEG_BIG = -1e30
SAMPLE_ROWS = 16
ATT_SUB = 128
ROW_BLOCK = 256
VMEM_LIMIT = 56 * 1024 * 1024


def _cparams(semantics):
    return pltpu.CompilerParams(dimension_semantics=semantics, vmem_limit_bytes=VMEM_LIMIT)


def _dot(a, b):
    return jnp.dot(a, b, preferred_element_type=F32)


def _dot_nt(a, b):
    return lax.dot_general(a, b, (((1,), (1,)), ((), ())), preferred_element_type=F32)


def _sigmoid(x):
    return 1.0 / (1.0 + jnp.exp(-x))


def _rope(x, cos, sin, half):
    n = x.shape[-1]
    lane = lax.broadcasted_iota(I32, x.shape, 1)
    first = (lane & (2 * half - 1)) < half
    partner = jnp.where(first, pltpu.roll(x, n - half, 1), pltpu.roll(x, half, 1))
    reps = n // cos.shape[-1]
    if reps > 1:
        cos = jnp.tile(cos, (1, reps))
        sin = jnp.tile(sin, (1, reps))
    return x * cos + partner * sin


def _rope_tables(pos, head_dim):
    half = head_dim // 2
    inv = ROPE_THETA ** (-jnp.arange(half, dtype=F32) / half)
    ang = pos.astype(F32)[:, None] * inv[None, :]
    cos = jnp.cos(ang)
    sin = jnp.sin(ang)
    cos = jnp.concatenate([cos, cos], -1)
    sin = jnp.concatenate([-sin, sin], -1)
    reps = 128 // head_dim
    return jnp.tile(cos, (1, reps)), jnp.tile(sin, (1, reps))


def _wspec(k, tn, layer, col_block):
    return pl.BlockSpec((None, k, tn), lambda i, j: (layer, 0, col_block + j))


def _mm_kernel(x_ref, w_ref, o_ref):
    o_ref[...] = _dot(x_ref[...], w_ref[...]).astype(o_ref.dtype)


def _proj_lru(x, w_in, layer, tm, tn):
    m, k = x.shape
    n = 2 * D_LRU
    return pl.pallas_call(
        _mm_kernel,
        grid=(m // tm, n // tn),
        in_specs=[pl.BlockSpec((tm, k), lambda i, j: (i, 0)),
                  _wspec(k, tn, layer, O_LRU // tn)],
        out_specs=pl.BlockSpec((tm, tn), lambda i, j: (i, j)),
        out_shape=jax.ShapeDtypeStruct((m, n), F32),
        compiler_params=_cparams(("parallel", "arbitrary")),
        name="proj_lru",
    )(x, w_in)


def _qkv_kernel(x_ref, w_ref, cos_ref, sin_ref, q_ref, kf_ref, kb_ref, vf_ref, vb_ref):
    j = pl.program_id(1)
    tm = x_ref.shape[0]
    blocks = [slice(r, r + min(tm, ROW_BLOCK)) for r in range(0, tm, ROW_BLOCK)]

    @pl.when(j == 0)
    def _():
        for rows in blocks:
            p = _dot(x_ref[rows, :], w_ref[...])
            q = _rope(p, cos_ref[rows, :], sin_ref[rows, :], HEAD_DIM // 2)
            q_ref[rows, :] = (q * Q_SCALE).astype(BF16)

    @pl.when(j == 1)
    def _():
        for rows in blocks:
            p = _dot(x_ref[rows, :], w_ref[...])
            k = _rope(p, cos_ref[rows, :], sin_ref[rows, :], HEAD_DIM // 2)
            kf_ref[rows, :] = k
            kb_ref[rows, :] = k.astype(BF16)

    @pl.when(j == 2)
    def _():
        for rows in blocks:
            p = _dot(x_ref[rows, :], w_ref[...])
            vf_ref[rows, :] = p
            vb_ref[rows, :] = p.astype(BF16)


def _proj_qkv(x, w_in, cos, sin, layer, tm):
    m, k = x.shape
    nt = cos.shape[0] // tm
    row = lambda i, j: (i, 0)
    out_blk = pl.BlockSpec((tm, D_ATT), row)
    sds = lambda dt: jax.ShapeDtypeStruct((m, D_ATT), dt)
    return pl.pallas_call(
        _qkv_kernel,
        grid=(m // tm, 3),
        in_specs=[pl.BlockSpec((tm, k), row),
                  _wspec(k, D_ATT, layer, O_QKV // D_ATT),
                  pl.BlockSpec((tm, 128), lambda i, j: (i % nt, 0)),
                  pl.BlockSpec((tm, 128), lambda i, j: (i % nt, 0))],
        out_specs=[out_blk] * 5,
        out_shape=[sds(BF16), sds(F32), sds(BF16), sds(F32), sds(BF16)],
        compiler_params=_cparams(("parallel", "arbitrary")),
        name="proj_qkv",
    )(x, w_in, cos, sin)


def _qi_kernel(x_ref, w_ref, cos_ref, sin_ref, o_ref):
    p = _dot(x_ref[...], w_ref[...])
    o_ref[...] = _rope(p, cos_ref[...], sin_ref[...], IDX_DIM // 2).astype(BF16)


def _proj_qi(x, w_in, cos, sin, layer, tm, tn):
    m, k = x.shape
    nt = cos.shape[0] // tm
    return pl.pallas_call(
        _qi_kernel,
        grid=(m // tm, D_QIDX // tn),
        in_specs=[pl.BlockSpec((tm, k), lambda i, j: (i, 0)),
                  _wspec(k, tn, layer, O_QI // tn),
                  pl.BlockSpec((tm, 128), lambda i, j: (i % nt, 0)),
                  pl.BlockSpec((tm, 128), lambda i, j: (i % nt, 0))],
        out_specs=pl.BlockSpec((tm, tn), lambda i, j: (i, j)),
        out_shape=jax.ShapeDtypeStruct((m, D_QIDX), BF16),
        compiler_params=_cparams(("parallel", "arbitrary")),
        name="proj_qi",
    )(x, w_in, cos, sin)


def _kiwi_kernel(x_ref, w_ref, cos_ref, sin_ref, of_ref, ob_ref):
    p = _dot(x_ref[...], w_ref[...])
    r = _rope(p, cos_ref[...], sin_ref[...], IDX_DIM // 2)
    lane = lax.broadcasted_iota(I32, p.shape, 1)
    of_ref[...] = jnp.where(lane < IDX_DIM, r,
                            jnp.where(lane < IDX_DIM + IDX_HEADS, p * WI_SCALE, 0.0))
    ob_ref[...] = jnp.where(lane < IDX_DIM, r, pltpu.roll(r, IDX_DIM, 1)).astype(BF16)


def _proj_kiwi(x, w_kiwi, cos, sin, layer, tm):
    m, k = x.shape
    nt = cos.shape[0] // tm
    blk = pl.BlockSpec((tm, KIWI_W), lambda i: (i, 0))
    return pl.pallas_call(
        _kiwi_kernel,
        grid=(m // tm,),
        in_specs=[pl.BlockSpec((tm, k), lambda i: (i, 0)),
                  pl.BlockSpec((None, k, KIWI_W), lambda i: (layer, 0, 0)),
                  pl.BlockSpec((tm, 128), lambda i: (i % nt, 0)),
                  pl.BlockSpec((tm, 128), lambda i: (i % nt, 0))],
        out_specs=[blk, blk],
        out_shape=[jax.ShapeDtypeStruct((m, KIWI_W), F32), jax.ShapeDtypeStruct((m, KIWI_W), BF16)],
        compiler_params=_cparams(("parallel",)),
        name="proj_kiwi",
    )(x, w_kiwi, cos, sin)


def _gelu_tanh(x):
    return 0.5 * x * (1.0 + jnp.tanh(0.7978845608028654 * (x + 0.044715 * (x * x * x))))


def _lru_coeffs(xc, ga, gx, gab, gxb, lam):
    r = _sigmoid(ga + gab)
    i = _sigmoid(gx + gxb)
    z = -lam
    softplus = jnp.maximum(z, 0.0) + jnp.log(1.0 + jnp.exp(-jnp.abs(z)))
    log_a = (-LRU_C * softplus) * r
    a = jnp.exp(log_a)
    b = jnp.sqrt(1.0 - a * a) * (i * xc)
    return a, b


def _lru_prompt_kernel(xb_ref, gt_ref, cw_ref, cb_ref, gw_ref, gab_ref, gxb_ref, lam_ref,
                       y_ref, hl_ref, cl_ref, xext_ref, h_ref, *, tc):
    t = pl.program_id(2)

    @pl.when(t == 0)
    def _():
        xext_ref[0:8, :] = jnp.zeros((8, LRU_BW), F32)
        h_ref[...] = jnp.zeros_like(h_ref)

    x = xb_ref[...]
    xext_ref[8:8 + tc, :] = x
    cw = cw_ref[...]
    xc = (cb_ref[...] + cw[3:4] * x + cw[2:3] * xext_ref[7:7 + tc, :]
          + cw[1:2] * xext_ref[6:6 + tc, :] + cw[0:1] * xext_ref[5:5 + tc, :])
    cl_ref[0] = xext_ref[tc + 5:tc + 8, :]
    xext_ref[0:8, :] = x[tc - 8:tc, :]

    g = _dot(xc.astype(BF16), gw_ref[0])
    a, b = _lru_coeffs(xc, g[:, :LRU_BW], g[:, LRU_BW:], gab_ref[...], gxb_ref[...], lam_ref[...])

    row = lax.broadcasted_iota(I32, (tc, LRU_BW), 0)
    d = 1
    while d < tc:
        a_sh = pltpu.roll(a, d, 0)
        b_sh = pltpu.roll(b, d, 0)
        live = row >= d
        b = jnp.where(live, a * b_sh + b, b)
        a = jnp.where(live, a * a_sh, a)
        d *= 2
    h = a * h_ref[...] + b
    h_ref[...] = h[tc - 1:tc, :]
    hl_ref[0] = h[tc - 1:tc, :]
    y_ref[...] = (h * _gelu_tanh(gt_ref[...])).astype(BF16)


def _lru_prompt(xg, conv_w, conv_b, gw, gab, gxb, lam, layer, batch, seq, tc):
    nt = seq // tc
    vec = pl.BlockSpec((1, LRU_BW), lambda b, n, t: (0, n))
    y, h_last, conv_last = pl.pallas_call(
        functools.partial(_lru_prompt_kernel, tc=tc),
        grid=(batch, LRU_BLOCKS, nt),
        in_specs=[pl.BlockSpec((tc, LRU_BW), lambda b, n, t: (b * nt + t, n)),
                  pl.BlockSpec((tc, LRU_BW), lambda b, n, t: (b * nt + t, LRU_BLOCKS + n)),
                  pl.BlockSpec((CONV_W, LRU_BW), lambda b, n, t: (0, n)),
                  vec,
                  pl.BlockSpec((None, 1, LRU_BW, 2 * LRU_BW), lambda b, n, t: (layer, n, 0, 0)),
                  vec, vec, vec],
        out_specs=[pl.BlockSpec((tc, LRU_BW), lambda b, n, t: (b * nt + t, n)),
                   pl.BlockSpec((1, 1, LRU_BW), lambda b, n, t: (b, 0, n)),
                   pl.BlockSpec((1, CONV_W - 1, LRU_BW), lambda b, n, t: (b, 0, n))],
        out_shape=[jax.ShapeDtypeStruct((batch * seq, D_LRU), BF16),
                   jax.ShapeDtypeStruct((batch, 1, D_LRU), F32),
                   jax.ShapeDtypeStruct((batch, CONV_W - 1, D_LRU), F32)],
        scratch_shapes=[pltpu.VMEM((tc + 8, LRU_BW), F32), pltpu.VMEM((1, LRU_BW), F32)],
        compiler_params=_cparams(("parallel", "parallel", "arbitrary")),
        name="rglru_prompt",
    )(xg, xg, conv_w, conv_b, gw, gab, gxb, lam)
    return y, h_last[:, 0], conv_last


def _lru_sample_kernel(xg_ref, c0_ref, c1_ref, c2_ref, h0_ref, cw_ref, cb_ref, gw_ref,
                       gab_ref, gxb_ref, lam_ref, y_ref, h_ref):
    xb = xg_ref[:, :D_LRU]
    gate = xg_ref[:, D_LRU:]
    cw = cw_ref[...]
    xc = (cb_ref[...] + cw[0:1] * c0_ref[...] + cw[1:2] * c1_ref[...]
          + cw[2:3] * c2_ref[...] + cw[3:4] * xb)
    xcb = xc.astype(BF16)
    ga, gx = [], []
    for n in range(LRU_BLOCKS):
        g = _dot(xcb[:, n * LRU_BW:(n + 1) * LRU_BW], gw_ref[n])
        ga.append(g[:, :LRU_BW])
        gx.append(g[:, LRU_BW:])
    a, b = _lru_coeffs(xc, jnp.concatenate(ga, -1), jnp.concatenate(gx, -1),
                       gab_ref[...], gxb_ref[...], lam_ref[...])
    h = a * h0_ref[...] + b
    h_ref[...] = h
    y_ref[...] = (h * _gelu_tanh(gate)).astype(BF16)


def _lru_sample(xg, conv_state, h0, conv_w, conv_b, gw_l, gab, gxb, lam):
    rows = xg.shape[0]
    return pl.pallas_call(
        _lru_sample_kernel,
        out_shape=[jax.ShapeDtypeStruct((rows, D_LRU), BF16), jax.ShapeDtypeStruct((rows, D_LRU), F32)],
        compiler_params=pltpu.CompilerParams(vmem_limit_bytes=VMEM_LIMIT),
        name="rglru_sample",
    )(xg, conv_state[:, 0], conv_state[:, 1], conv_state[:, 2], h0, conv_w, conv_b, gw_l, gab, gxb, lam)


def _sortable_key(s):
    s = jnp.where(s == 0.0, 0.0, s)
    bits = pltpu.bitcast(s, I32)
    return bits ^ ((bits >> 31) & 0x7FFFFFFF)


def _bit_transpose32(words):
    a = list(words)
    j = 16
    m = 0x0000FFFF
    while j:
        k = 0
        while k < 32:
            t = (a[k] ^ lax.shift_right_logical(a[k + j], j)) & m
            a[k] = a[k] ^ t
            a[k + j] = a[k + j] ^ jnp.left_shift(t, j)
            k = (k + j + 1) & ~j
        j >>= 1
        m = (m ^ (m << j)) & 0xFFFFFFFF
    return a


def _attn_prompt_kernel(q_ref, qi_ref, wt_ref, k_ref, vt_ref, kd_ref, o_ref,
                        keys_ref, planes_ref, cand_ref, bias_ref, qm_ref, acc_ref, ml_ref, *, tq):
    i = pl.program_id(1)
    nk = i + 1
    ck = tq

    lane = lax.broadcasted_iota(I32, (tq, 128), 1)
    for p in range(IDX_HEADS // 2):
        slab = qi_ref[:, p * 128:(p + 1) * 128]
        zero = jnp.zeros_like(slab)
        qm_ref[2 * p] = jnp.where(lane < IDX_DIM, slab, zero)
        qm_ref[2 * p + 1] = jnp.where(lane >= IDX_DIM, slab, zero)

    qpos = i * tq + lax.broadcasted_iota(I32, (1, tq), 1)
    kidx = lax.broadcasted_iota(I32, (ck, tq), 0)

    def score_body(c, carry):
        off = pl.multiple_of(c * ck, ck)
        kd = kd_ref[0, pl.ds(off, ck), :]
        acc = jnp.zeros((ck, tq), F32)
        for h in range(IDX_HEADS):
            x = _dot_nt(kd, qm_ref[h])
            acc = acc + wt_ref[0, h:h + 1, :] * jnp.maximum(x, 0.0)
        key = jnp.where(kidx + off <= qpos, _sortable_key(acc), INT_MIN)
        keys_ref[pl.ds(off, ck), :] = key
        ukey = key ^ INT_MIN
        planes = _bit_transpose32([ukey[r * 8:(r + 1) * 8, :] for r in range(32)])
        for b in range(32):
            planes_ref[c, b] = planes[b]
        cand_ref[c] = jnp.full((8, tq), -1, I32)
        return carry

    lax.fori_loop(0, nk, score_body, 0)

    kk = jnp.minimum(qpos + 1, TOPK)

    def sweep(b, drop, first):
        def body(c, acc):
            cand = cand_ref[c]
            if not first:
                cand = cand & (planes_ref[c, b - 1] ^ drop)
                cand_ref[c] = cand
            return acc + lax.population_count(cand & planes_ref[c, b])
        acc = lax.fori_loop(0, nk, body, jnp.zeros((8, tq), I32))
        return acc.sum(axis=0, keepdims=True)

    def decide(b, n_ones, rem, t_u):
        take = n_ones >= rem
        t_u = jnp.where(take, t_u | jnp.left_shift(1, 31 - b), t_u)
        rem = jnp.where(take, rem, rem - n_ones)
        return rem, t_u, jnp.where(take, 0, -1)

    rem, t_u, drop = decide(0, sweep(0, None, True), kk, jnp.zeros((1, tq), I32))

    def bit_body(b, carry):
        rem, t_u, drop = carry
        return decide(b, sweep(b, drop, False), rem, t_u)

    rem, t_u, drop = lax.fori_loop(1, 32, bit_body, (rem, t_u, drop))

    def last_body(c, acc):
        return acc + lax.population_count(cand_ref[c] & (planes_ref[c, 31] ^ drop))

    n_eq = lax.fori_loop(0, nk, last_body, jnp.zeros((8, tq), I32)).sum(axis=0, keepdims=True)
    t = t_u ^ INT_MIN
    need = rem

    def count(pred):
        def body(c, acc):
            off = pl.multiple_of(c * ck, ck)
            m = jnp.where(pred(keys_ref[pl.ds(off, ck), :], off), 1, 0)
            return acc + m.reshape(ck // 8, 8, tq).sum(axis=0)
        acc = lax.fori_loop(0, nk, body, jnp.zeros((8, tq), I32))
        return acc.sum(axis=0, keepdims=True)

    tie = jnp.max(n_eq - need) > 0
    jbits = 13

    def j_body(it, jcut):
        j_try = jcut + jnp.left_shift(1, jbits - 1 - it)
        n = count(lambda kc, off: (kc == t) & (kidx + off < j_try))
        return jnp.where(n <= need, j_try, jcut)

    jcut = lax.fori_loop(0, jnp.where(tie, jbits, 0), j_body,
                         jnp.where(tie, 0, 2 ** jbits) + jnp.zeros((1, tq), I32))

    def bias_body(c, carry):
        off = pl.multiple_of(c * ck, ck)
        kc = keys_ref[pl.ds(off, ck), :]
        sel = (kc > t) | ((kc == t) & (kidx + off < jcut))
        bias_ref[pl.ds(off, ck), :] = jnp.where(sel, 0.0, NEG_BIG)
        return carry

    lax.fori_loop(0, nk, bias_body, 0)

    ml_ref[0:N_HEADS, :] = jnp.full((N_HEADS, tq), NEG_BIG, F32)
    ml_ref[N_HEADS:, :] = jnp.zeros((N_HEADS, tq), F32)
    acc_ref[...] = jnp.zeros_like(acc_ref)

    def att_body(c, carry):
        for j in range(ck // ATT_SUB):
            off = pl.multiple_of(c * ck + j * ATT_SUB, ATT_SUB)
            bias = bias_ref[pl.ds(off, ATT_SUB), :]
            for h in range(N_HEADS):
                hs = slice(h * HEAD_DIM, (h + 1) * HEAD_DIM)
                s = _dot_nt(k_ref[0, pl.ds(off, ATT_SUB), hs], q_ref[:, hs]) + bias
                m_old = ml_ref[h:h + 1, :]
                m_new = jnp.maximum(m_old, s.max(axis=0, keepdims=True))
                alpha = jnp.exp2(m_old - m_new)
                p = jnp.exp2(s - m_new)
                ml_ref[h:h + 1, :] = m_new
                ml_ref[N_HEADS + h:N_HEADS + h + 1, :] = (
                    ml_ref[N_HEADS + h:N_HEADS + h + 1, :] * alpha + p.sum(axis=0, keepdims=True))
                pv = _dot(vt_ref[0, hs, pl.ds(off, ATT_SUB)], p.astype(BF16))
                acc_ref[h] = acc_ref[h] * alpha + pv
        return carry

    lax.fori_loop(0, nk, att_body, 0)

    for h in range(N_HEADS):
        out = acc_ref[h] / ml_ref[N_HEADS + h:N_HEADS + h + 1, :]
        o_ref[:, h * HEAD_DIM:(h + 1) * HEAD_DIM] = out.T.astype(BF16)


def _attn_prompt(q, qi, wt, k, vt, kd, batch, seq, tq):
    nq = seq // tq
    rows = lambda b, i: (b * nq + i, 0)
    whole = lambda b, i: (b, 0, 0)
    return pl.pallas_call(
        functools.partial(_attn_prompt_kernel, tq=tq),
        grid=(batch, nq),
        in_specs=[pl.BlockSpec((tq, D_ATT), rows),
                  pl.BlockSpec((tq, D_QIDX), rows),
                  pl.BlockSpec((1, IDX_HEADS, tq), lambda b, i: (b, 0, i)),
                  pl.BlockSpec((1, seq, D_ATT), whole),
                  pl.BlockSpec((1, D_ATT, seq), whole),
                  pl.BlockSpec((1, seq, KIWI_W), whole)],
        out_specs=pl.BlockSpec((tq, D_ATT), rows),
        out_shape=jax.ShapeDtypeStruct((batch * seq, D_ATT), BF16),
        scratch_shapes=[pltpu.VMEM((seq, tq), I32),
                        pltpu.VMEM((nq, 32, 8, tq), I32),
                        pltpu.VMEM((nq, 8, tq), I32),
                        pltpu.VMEM((seq, tq), F32),
                        pltpu.VMEM((IDX_HEADS, tq, 128), BF16),
                        pltpu.VMEM((N_HEADS, HEAD_DIM, tq), F32),
                        pltpu.VMEM((2 * N_HEADS, tq), F32)],
        compiler_params=_cparams(("parallel", "arbitrary")),
        name="dsa_prompt",
    )(q, qi, wt, k, vt, kd)


def _select_sample_kernel(pt_ref, qi_ref, w_ref, kn_ref, cache_ref, idx_ref,
                          buf_ref, sem, sc_ref, *, layer, n_pages):
    b = pl.program_id(0)
    past = n_pages * PAGE

    def page_copy(p):
        return pltpu.make_async_copy(cache_ref.at[layer, pt_ref[b, p]], buf_ref.at[p], sem)

    def start(p, c):
        page_copy(p).start()
        return c

    lax.fori_loop(0, n_pages, start, 0)

    def wait(p, c):
        page_copy(p).wait()
        return c

    lax.fori_loop(0, n_pages, wait, 0)

    qi = qi_ref[0]
    w = w_ref[0]
    group = 8

    def score_body(g, c):
        for j in range(group):
            p = g * group + j
            x = _dot(qi, buf_ref[p].astype(BF16))
            sc_ref[pl.ds(p, 1), :] = (w * jnp.maximum(x, 0.0)).sum(axis=0, keepdims=True)
        return c

    lax.fori_loop(0, n_pages // group, score_body, 0)

    kn = kn_ref[0].astype(BF16).astype(F32)
    x_new = (qi.astype(F32) * kn).sum(axis=1, keepdims=True)
    s_new = (w * jnp.maximum(x_new, 0.0)).sum(axis=0, keepdims=True)

    keys = _sortable_key(sc_ref[...])
    key_new = _sortable_key(s_new)
    pos = (lax.broadcasted_iota(I32, (n_pages, PAGE), 0) * PAGE
           + lax.broadcasted_iota(I32, (n_pages, PAGE), 1))

    def count(m_past, m_new):
        return jnp.sum(jnp.where(m_past, 1, 0)) + jnp.sum(jnp.where(m_new, 1, 0))

    t0 = jnp.where(count(keys >= 0, key_new >= 0) >= TOPK, 0, INT_MIN)

    def bit_body(it, t):
        t_try = t | jnp.left_shift(1, 30 - it)
        return jnp.where(count(keys >= t_try, key_new >= t_try) >= TOPK, t_try, t)

    t = lax.fori_loop(0, 31, bit_body, t0)
    need = TOPK - count(keys > t, key_new > t)
    tie = count(keys == t, key_new == t) > need
    jbits = 15

    def j_body(it, jcut):
        j_try = jcut + jnp.left_shift(1, jbits - 1 - it)
        n = count((keys == t) & (pos < j_try), (key_new == t) & (past < j_try))
        return jnp.where(n <= need, j_try, jcut)

    jcut = lax.fori_loop(0, jnp.where(tie, jbits, 0), j_body, jnp.where(tie, 0, 2 ** jbits))
    sel = (keys > t) | ((keys == t) & (pos < jcut))
    n_past = jnp.sum(jnp.where(sel, 1, 0))

    selb = jnp.where(sel, 1.0, 0.0).astype(BF16)
    r_i = lax.broadcasted_iota(I32, (PAGE, PAGE), 0)
    c_i = lax.broadcasted_iota(I32, (PAGE, PAGE), 1)
    incl = _dot(selb, jnp.where(r_i <= c_i, 1.0, 0.0).astype(BF16))
    cnt = jnp.broadcast_to(incl[:, PAGE - 1:PAGE], (n_pages, TOPK))
    pr_i = lax.broadcasted_iota(I32, (n_pages, n_pages), 0)
    pc_i = lax.broadcasted_iota(I32, (n_pages, n_pages), 1)
    pinc = _dot(jnp.where(pc_i <= pr_i, 1.0, 0.0).astype(BF16), cnt.astype(BF16))
    slot = lax.broadcasted_iota(I32, (n_pages, TOPK), 1).astype(F32)
    before = pinc <= slot
    page = jnp.sum(jnp.where(before, 1, 0), axis=0, keepdims=True)
    pexc = jnp.sum(jnp.where(before, cnt, 0.0), axis=0, keepdims=True)
    rank = slot[0:1, :] - pexc
    onehot = jnp.where(lax.broadcasted_iota(I32, (n_pages, TOPK), 0) == page, 1.0, 0.0).astype(BF16)
    incl_sel = _dot(incl.T.astype(BF16), onehot)
    off = jnp.sum(jnp.where(incl_sel <= rank, 1, 0), axis=0, keepdims=True)
    slot_i = lax.broadcasted_iota(I32, (1, TOPK), 1)
    idx_ref[0] = jnp.where(slot_i < n_past, page * PAGE + off, past)


def _select_sample(page_table, qi, w, ki_new, cache_kidx_t, layer):
    bd, n_pages = page_table.shape
    grid_spec = pltpu.PrefetchScalarGridSpec(
        num_scalar_prefetch=1,
        grid=(bd,),
        in_specs=[pl.BlockSpec((1, IDX_HEADS, IDX_DIM), lambda b, pt: (b, 0, 0)),
                  pl.BlockSpec((1, IDX_HEADS, 1), lambda b, pt: (b, 0, 0)),
                  pl.BlockSpec((1, 1, IDX_DIM), lambda b, pt: (b, 0, 0)),
                  pl.BlockSpec(memory_space=pl.ANY)],
        out_specs=pl.BlockSpec((1, 1, TOPK), lambda b, pt: (b, 0, 0)),
        scratch_shapes=[pltpu.VMEM((n_pages, IDX_DIM, PAGE), F32),
                        pltpu.SemaphoreType.DMA(()),
                        pltpu.VMEM((n_pages, PAGE), F32)],
    )
    return pl.pallas_call(
        functools.partial(_select_sample_kernel, layer=layer, n_pages=n_pages),
        grid_spec=grid_spec,
        out_shape=jax.ShapeDtypeStruct((bd, 1, TOPK), I32),
        compiler_params=_cparams(("arbitrary",)),
        name="dsa_sample_select",
    )(page_table, qi, w, ki_new, cache_kidx_t)


def _attn_sample_kernel(pt_ref, idx_ref, q_ref, kn_ref, vn_ref, ck_ref, cv_ref, o_ref,
                        kbuf_ref, vbuf_ref, sems, *, layer, n_pages):
    b = pl.program_id(0)
    past = n_pages * PAGE

    def row_copies(r):
        s = jnp.minimum(idx_ref[b, r], past - 1)
        page = pt_ref[b, s // PAGE]
        off = s % PAGE
        return (pltpu.make_async_copy(ck_ref.at[layer, page, off], kbuf_ref.at[r], sems.at[0]),
                pltpu.make_async_copy(cv_ref.at[layer, page, off], vbuf_ref.at[r], sems.at[1]))

    def start(r, c):
        ck, cv = row_copies(r)
        ck.start()
        cv.start()
        return c

    lax.fori_loop(0, TOPK, start, 0)

    def wait(r, c):
        ck, cv = row_copies(r)
        ck.wait()
        cv.wait()
        return c

    lax.fori_loop(0, TOPK, wait, 0)

    @pl.when(idx_ref[b, TOPK - 1] == past)
    def _():
        kbuf_ref[TOPK - 1] = kn_ref[0]
        vbuf_ref[TOPK - 1] = vn_ref[0]

    q = q_ref[...]
    logits = (kbuf_ref[...] * q).sum(axis=-1, keepdims=True)
    m = logits.max(axis=0, keepdims=True)
    p = jnp.exp2(logits - m)
    den = p.sum(axis=0, keepdims=True)
    o_ref[...] = (p * vbuf_ref[...]).sum(axis=0, keepdims=True) / den


def _attn_sample(page_table, idx, q, k_new, v_new, cache_k, cache_v, layer):
    bd, n_pages = page_table.shape
    row = pl.BlockSpec((1, N_HEADS, HEAD_DIM), lambda b, pt, ix: (b, 0, 0))
    grid_spec = pltpu.PrefetchScalarGridSpec(
        num_scalar_prefetch=2,
        grid=(bd,),
        in_specs=[row, row, row, pl.BlockSpec(memory_space=pl.ANY), pl.BlockSpec(memory_space=pl.ANY)],
        out_specs=row,
        scratch_shapes=[pltpu.VMEM((TOPK, N_HEADS, HEAD_DIM), F32),
                        pltpu.VMEM((TOPK, N_HEADS, HEAD_DIM), F32),
                        pltpu.SemaphoreType.DMA((2,))],
    )
    return pl.pallas_call(
        functools.partial(_attn_sample_kernel, layer=layer, n_pages=n_pages),
        grid_spec=grid_spec,
        out_shape=jax.ShapeDtypeStruct((bd, N_HEADS, HEAD_DIM), F32),
        compiler_params=_cparams(("arbitrary",)),
        name="dsa_sample_attend",
    )(page_table, idx, q, k_new, v_new, cache_k, cache_v)


def _layer_norm(z, g, b):
    mu = jnp.mean(z, axis=-1, keepdims=True)
    zc = z - mu
    var = jnp.mean(zc * zc, axis=-1, keepdims=True)
    return zc * lax.rsqrt(var + LN_EPS) * g + b


def _mix_ln_kernel(yl_ref, ya_ref, w_ref, x_ref, g_ref, b_ref, of_ref, ob_ref, *, alpha):
    mix = _dot(yl_ref[...], w_ref[:D_LRU, :]) + _dot(ya_ref[...], w_ref[D_LRU:, :])
    y = _layer_norm(alpha * x_ref[...] + mix, g_ref[...], b_ref[...])
    of_ref[...] = y
    ob_ref[...] = y.astype(BF16)


def _mix_ln(y_lru, y_att, w_out, x, g, b, layer, tm, alpha):
    m = x.shape[0]
    row = lambda i: (i, 0)
    fixed = lambda i: (0, 0)
    return pl.pallas_call(
        functools.partial(_mix_ln_kernel, alpha=alpha),
        grid=(m // tm,),
        in_specs=[pl.BlockSpec((tm, D_LRU), row), pl.BlockSpec((tm, D_ATT), row),
                  pl.BlockSpec((None, D_LRU + D_ATT, D_MODEL), lambda i: (layer, 0, 0)),
                  pl.BlockSpec((tm, D_MODEL), row),
                  pl.BlockSpec((1, D_MODEL), fixed), pl.BlockSpec((1, D_MODEL), fixed)],
        out_specs=[pl.BlockSpec((tm, D_MODEL), row), pl.BlockSpec((tm, D_MODEL), row)],
        out_shape=[jax.ShapeDtypeStruct((m, D_MODEL), F32), jax.ShapeDtypeStruct((m, D_MODEL), BF16)],
        compiler_params=_cparams(("parallel",)),
        name="mix_ln1",
    )(y_lru, y_att, w_out, x, g, b)


def _ffn_up_kernel(x_ref, wg_ref, wu_ref, o_ref):
    x = x_ref[...]
    g = _dot(x, wg_ref[...])
    u = _dot(x, wu_ref[...])
    o_ref[...] = (g * _sigmoid(g) * u).astype(BF16)


def _ffn_up(x, wg, wu, layer, tm, tn):
    m, k = x.shape
    return pl.pallas_call(
        _ffn_up_kernel,
        grid=(m // tm, D_FF // tn),
        in_specs=[pl.BlockSpec((tm, k), lambda i, j: (i, 0)),
                  _wspec(k, tn, layer, 0),
                  _wspec(k, tn, layer, 0)],
        out_specs=pl.BlockSpec((tm, tn), lambda i, j: (i, j)),
        out_shape=jax.ShapeDtypeStruct((m, D_FF), BF16),
        compiler_params=_cparams(("parallel", "arbitrary")),
        name="ffn_up",
    )(x, wg, wu)


def _ffn_down_ln_kernel(h_ref, w_ref, x_ref, g_ref, b_ref, of_ref, ob_ref, *, alpha):
    k = pl.program_id(1)
    last = pl.num_programs(1) - 1
    tm = h_ref.shape[0]
    blocks = [slice(r, r + min(tm, ROW_BLOCK)) for r in range(0, tm, ROW_BLOCK)]

    @pl.when(k == 0)
    def _():
        for rows in blocks:
            of_ref[rows, :] = _dot(h_ref[rows, :], w_ref[...])

    @pl.when((k > 0) & (k < last))
    def _():
        for rows in blocks:
            of_ref[rows, :] += _dot(h_ref[rows, :], w_ref[...])

    @pl.when(k == last)
    def _():
        for rows in blocks:
            z = alpha * x_ref[rows, :] + (of_ref[rows, :] + _dot(h_ref[rows, :], w_ref[...]))
            y = _layer_norm(z, g_ref[...], b_ref[...])
            of_ref[rows, :] = y
            ob_ref[rows, :] = y.astype(BF16)


def _ffn_down_ln(h, wd, x, g, b, layer, tm, tk, alpha):
    m = x.shape[0]
    row = lambda i, k: (i, 0)
    fixed = lambda i, k: (0, 0)
    assert D_FF // tk >= 2
    return pl.pallas_call(
        functools.partial(_ffn_down_ln_kernel, alpha=alpha),
        grid=(m // tm, D_FF // tk),
        in_specs=[pl.BlockSpec((tm, tk), lambda i, k: (i, k)),
                  pl.BlockSpec((None, tk, D_MODEL), lambda i, k: (layer, k, 0)),
                  pl.BlockSpec((tm, D_MODEL), row),
                  pl.BlockSpec((1, D_MODEL), fixed), pl.BlockSpec((1, D_MODEL), fixed)],
        out_specs=[pl.BlockSpec((tm, D_MODEL), row), pl.BlockSpec((tm, D_MODEL), row)],
        out_shape=[jax.ShapeDtypeStruct((m, D_MODEL), F32), jax.ShapeDtypeStruct((m, D_MODEL), BF16)],
        compiler_params=_cparams(("parallel", "arbitrary")),
        name="ffn_down_ln2",
    )(h, wd, x, g, b)


def _tiles(rows):
    if rows <= SAMPLE_ROWS:
        return dict(lru=rows, qkv=rows, qi=rows, kiwi=rows, mix=rows, up=rows, down=rows)
    return dict(lru=1024, qkv=1024, qi=1024, kiwi=1024, mix=512, up=1024, down=1024)


def _project(xb16, w, layer, cos128, sin128, cos64, sin64):
    t = _tiles(xb16.shape[0])
    xg = _proj_lru(xb16, w["in"], layer, t["lru"], 1024)
    q, kf, kb, vf, vb = _proj_qkv(xb16, w["in"], cos128, sin128, layer, t["qkv"])
    qi = _proj_qi(xb16, w["in"], cos64, sin64, layer, t["qi"], 1024)
    kiwi_f, kiwi_b = _proj_kiwi(xb16, w["kiwi"], cos64, sin64, layer, t["kiwi"])
    return xg, q, kf, kb, vf, vb, qi, kiwi_f, kiwi_b


def _finish(x, y_lru, y_att, w, ln, layer, alpha):
    t = _tiles(x.shape[0])
    x1, x1b = _mix_ln(y_lru, y_att, w["out"], x, ln[0], ln[1], layer, t["mix"], alpha)
    h = _ffn_up(x1b, w["ffn_gate"], w["ffn_up"], layer, t["up"], 512)
    return _ffn_down_ln(h, w["ffn_down"], x1, ln[2], ln[3], layer, t["down"], 512, alpha)


def kernel(x_prompt, x_sample, cache_k, cache_v, cache_kidx, state_lru_h, state_lru_conv, page_table,
           w_in, conv_w, conv_b, gate_a_w, gate_a_b, gate_x_w, gate_x_b, lru_lambda, w_out,
           ln1_g, ln1_b, w_ffn_gate, w_ffn_up, w_ffn_down, ln2_g, ln2_b):
    depth = w_in.shape[0]
    bp, tp, _ = x_prompt.shape
    bd = x_sample.shape[0]
    n_pages = page_table.shape[1]
    past = n_pages * PAGE
    alpha = (2.0 * depth) ** 0.25
    pad = SAMPLE_ROWS - bd

    pos_p = jnp.arange(tp, dtype=I32)
    pos_s = jnp.full((SAMPLE_ROWS,), past, I32)
    tabs_p = _rope_tables(pos_p, HEAD_DIM) + _rope_tables(pos_p, IDX_DIM)
    tabs_s = _rope_tables(pos_s, HEAD_DIM) + _rope_tables(pos_s, IDX_DIM)

    xp = x_prompt.reshape(bp * tp, D_MODEL)
    xs = jnp.pad(x_sample.reshape(bd, D_MODEL), ((0, pad), (0, 0)))
    xp16 = xp.astype(BF16)
    xs16 = xs.astype(BF16)

    d_in = w_in.shape[2]
    w = {
        "in": w_in.astype(BF16),
        "kiwi": jnp.pad(w_in[:, :, O_KIWI:], ((0, 0), (0, 0), (0, KIWI_W - (d_in - O_KIWI)))).astype(BF16),
        "out": w_out.astype(BF16),
        "ffn_gate": w_ffn_gate.astype(BF16),
        "ffn_up": w_ffn_up.astype(BF16),
        "ffn_down": w_ffn_down.astype(BF16),
    }
    gw = jnp.concatenate([gate_a_w, gate_x_w], axis=-1).astype(BF16)
    cache_kidx_t = jnp.swapaxes(cache_kidx, 2, 3)

    outs = {n: [] for n in ("kp", "vp", "kip", "hp", "cp", "ks", "vs", "kis", "hs", "cs")}
    for l in range(depth):
        ln = (ln1_g[l][None], ln1_b[l][None], ln2_g[l][None], ln2_b[l][None])
        lru_vecs = (conv_w[l], conv_b[l][None])
        gate_vecs = (gate_a_b[l][None], gate_x_b[l][None], lru_lambda[l][None])

        xg, q, kf, kb, vf, vb, qi, kiwi_f, kiwi_b = _project(xp16, w, l, *tabs_p)
        y_lru, h_last, conv_last = _lru_prompt(xg, *lru_vecs, gw, *gate_vecs, layer=l,
                                               batch=bp, seq=tp, tc=512)
        wt = jnp.swapaxes(kiwi_f.reshape(bp, tp, KIWI_W)[:, :, IDX_DIM:IDX_DIM + IDX_HEADS], 1, 2)
        vt = jnp.swapaxes(vb.reshape(bp, tp, D_ATT), 1, 2)
        y_att = _attn_prompt(q, qi, wt, kb.reshape(bp, tp, D_ATT), vt,
                             kiwi_b.reshape(bp, tp, KIWI_W), batch=bp, seq=tp, tq=256)
        xp, xp16 = _finish(xp, y_lru, y_att, w, ln, l, alpha)
        outs["kp"].append(kf.reshape(bp, tp, N_HEADS, HEAD_DIM))
        outs["vp"].append(vf.reshape(bp, tp, N_HEADS, HEAD_DIM))
        outs["kip"].append(kiwi_f[:, :IDX_DIM].reshape(bp, tp, IDX_DIM))
        outs["hp"].append(h_last)
        outs["cp"].append(conv_last)

        xg, q, kf, kb, vf, vb, qi, kiwi_f, kiwi_b = _project(xs16, w, l, *tabs_s)
        conv_state = jnp.pad(state_lru_conv[l], ((0, pad), (0, 0), (0, 0)))
        h0 = jnp.pad(state_lru_h[l], ((0, pad), (0, 0)))
        y_lru, h_new = _lru_sample(xg, conv_state, h0, *lru_vecs, gw[l], *gate_vecs)
        ki_new = kiwi_f[:bd, :IDX_DIM]
        idx = _select_sample(page_table, qi[:bd].reshape(bd, IDX_HEADS, IDX_DIM),
                             kiwi_f[:bd, IDX_DIM:IDX_DIM + IDX_HEADS, None], ki_new[:, None, :],
                             cache_kidx_t, l)
        k_new = kf[:bd].reshape(bd, N_HEADS, HEAD_DIM)
        v_new = vf[:bd].reshape(bd, N_HEADS, HEAD_DIM)
        att = _attn_sample(page_table, idx[:, 0], q[:bd].astype(F32).reshape(bd, N_HEADS, HEAD_DIM),
                           k_new, v_new, cache_k, cache_v, l)
        y_att = jnp.pad(att.reshape(bd, D_ATT), ((0, pad), (0, 0))).astype(BF16)
        xs, xs16 = _finish(xs, y_lru, y_att, w, ln, l, alpha)
        outs["ks"].append(k_new[:, None])
        outs["vs"].append(v_new[:, None])
        outs["kis"].append(ki_new[:, None])
        outs["hs"].append(h_new[:bd])
        outs["cs"].append(jnp.concatenate([state_lru_conv[l][:, 1:], xg[:bd, None, :D_LRU]], axis=1))

    stack = lambda n: jnp.stack(outs[n])
    return (xp.reshape(bp, tp, D_MODEL), xs[:bd].reshape(bd, 1, D_MODEL),
            stack("kp"), stack("vp"), stack("kip"), stack("hp"), stack("cp"),
            stack("ks"), stack("vs"), stack("kis"), stack("hs"), stack("cs"))
```

```python
import functools
import math

import jax
import jax.numpy as jnp
from jax import lax
from jax.experimental import pallas as pl
from jax.experimental.pallas import tpu as pltpu

F32 = jnp.float32
BF16 = jnp.bfloat16
I32 = jnp.int32

D_MODEL = 2048
D_LRU = 1024
LRU_BLOCKS = 8
LRU_BW = 128
CONV_W = 4
LRU_C = 8.0
N_HEADS = 8
HEAD_DIM = 128
D_ATT = N_HEADS * HEAD_DIM
IDX_HEADS = 16
IDX_DIM = 64
D_QIDX = IDX_HEADS * IDX_DIM
D_FF = 5632
TOPK = 256
ROPE_THETA = 10000.0
PAGE = 128
LN_EPS = 1e-5
KIWI_W = 128
WI_SCALE = float(D_QIDX) ** -0.5
Q_SCALE = float(HEAD_DIM) ** -0.5 * math.log2(math.e)

O_LRU = 0
O_QKV = 2 * D_LRU
O_QI = O_QKV + 3 * D_ATT
O_KIWI = O_QI + D_QIDX

INT_MIN = -(2 ** 31)
NEG_BIG = -1e30
SAMPLE_ROWS = 16
ATT_SUB = 128
ROW_BLOCK = 256
VMEM_LIMIT = 56 * 1024 * 1024


def _cparams(semantics):
    return pltpu.CompilerParams(dimension_semantics=semantics, vmem_limit_bytes=VMEM_LIMIT)


def _dot(a, b):
    return jnp.dot(a, b, preferred_element_type=F32)


def _dot_nt(a, b):
    return lax.dot_general(a, b, (((1,), (1,)), ((), ())), preferred_element_type=F32)


def _sigmoid(x):
    return 1.0 / (1.0 + jnp.exp(-x))


def _rope(x, cos, sin, half):
    n = x.shape[-1]
    lane = lax.broadcasted_iota(I32, x.shape, 1)
    first = (lane & (2 * half - 1)) < half
    partner = jnp.where(first, pltpu.roll(x, n - half, 1), pltpu.roll(x, half, 1))
    reps = n // cos.shape[-1]
    if reps > 1:
        cos = jnp.tile(cos, (1, reps))
        sin = jnp.tile(sin, (1, reps))
    return x * cos + partner * sin


def _rope_tables(pos, head_dim):
    half = head_dim // 2
    inv = ROPE_THETA ** (-jnp.arange(half, dtype=F32) / half)
    ang = pos.astype(F32)[:, None] * inv[None, :]
    cos = jnp.cos(ang)
    sin = jnp.sin(ang)
    cos = jnp.concatenate([cos, cos], -1)
    sin = jnp.concatenate([-sin, sin], -1)
    reps = 128 // head_dim
    return jnp.tile(cos, (1, reps)), jnp.tile(sin, (1, reps))


def _stage_weight(w_ref, wb_ref):
    rb = min(w_ref.shape[0], ROW_BLOCK)
    for r in range(0, w_ref.shape[0], rb):
        wb_ref[r:r + rb, :] = w_ref[r:r + rb, :].astype(BF16)


def _row_blocks(tm):
    return [slice(r, r + min(tm, ROW_BLOCK)) for r in range(0, tm, ROW_BLOCK)]


def _dual_specs(tm, k, ms, tn, layer, col_block, nt):
    tab_p = pl.BlockSpec((tm, 128), lambda j, i: (i % nt, 0))
    tab_s = pl.BlockSpec((ms, 128), lambda j, i: (0, 0))
    return [pl.BlockSpec((tm, k), lambda j, i: (i, 0)),
            pl.BlockSpec((ms, k), lambda j, i: (0, 0)),
            pl.BlockSpec((None, tn, k), lambda j, i: (layer, col_block + j, 0)),
            tab_p, tab_p, tab_s, tab_s]


def _proj_plain_kernel(xp_ref, xs_ref, w_ref, op_ref, os_ref, wb_ref):
    @pl.when(pl.program_id(1) == 0)
    def _():
        _stage_weight(w_ref, wb_ref)
        os_ref[...] = _dot_nt(xs_ref[...], wb_ref[...])

    for rows in _row_blocks(xp_ref.shape[0]):
        op_ref[rows, :] = _dot_nt(xp_ref[rows, :], wb_ref[...])


def _proj_lru(xp, xs, w_in_t, layer, tm, tn):
    mp, k = xp.shape
    ms = xs.shape[0]
    n = 2 * D_LRU
    return pl.pallas_call(
        _proj_plain_kernel,
        grid=(n // tn, mp // tm),
        in_specs=_dual_specs(tm, k, ms, tn, layer, O_LRU // tn, 1)[:3],
        out_specs=[pl.BlockSpec((tm, tn), lambda j, i: (i, j)),
                   pl.BlockSpec((ms, tn), lambda j, i: (0, j))],
        out_shape=[jax.ShapeDtypeStruct((mp, n), F32), jax.ShapeDtypeStruct((ms, n), F32)],
        scratch_shapes=[pltpu.VMEM((tn, k), BF16)],
        compiler_params=_cparams(("parallel", "arbitrary")),
        name="proj_lru",
    )(xp, xs, w_in_t)


def _emit_heads(p, cos, sin, half, scale, of_ref, ob_ref, rows):
    if half:
        p = _rope(p, cos, sin, half)
    if of_ref is not None:
        of_ref[rows, :] = p
    if scale != 1.0:
        p = p * scale
    ob_ref[rows, :] = p.astype(BF16)


def _proj_heads_kernel(xp_ref, xs_ref, w_ref, cp_ref, sp_ref, cs_ref, ss_ref, *rest,
                       half, scale, keep_f32):
    if keep_f32:
        ofp_ref, obp_ref, ofs_ref, obs_ref, wb_ref = rest
    else:
        (obp_ref, obs_ref, wb_ref), ofp_ref, ofs_ref = rest, None, None

    @pl.when(pl.program_id(1) == 0)
    def _():
        _stage_weight(w_ref, wb_ref)
        _emit_heads(_dot_nt(xs_ref[...], wb_ref[...]), cs_ref[...], ss_ref[...], half, scale,
                    ofs_ref, obs_ref, slice(None))

    for rows in _row_blocks(xp_ref.shape[0]):
        _emit_heads(_dot_nt(xp_ref[rows, :], wb_ref[...]), cp_ref[rows, :], sp_ref[rows, :],
                    half, scale, ofp_ref, obp_ref, rows)


def _proj_heads(xp, xs, w_in_t, tabs_p, tabs_s, layer, col, tm, half, scale, keep_f32, name):
    mp, k = xp.shape
    ms = xs.shape[0]
    tn = D_ATT
    nt = tabs_p[0].shape[0] // tm
    blk_p = pl.BlockSpec((tm, tn), lambda j, i: (i, 0))
    blk_s = pl.BlockSpec((ms, tn), lambda j, i: (0, 0))
    dts = (F32, BF16) if keep_f32 else (BF16,)
    return pl.pallas_call(
        functools.partial(_proj_heads_kernel, half=half, scale=scale, keep_f32=keep_f32),
        grid=(1, mp // tm),
        in_specs=_dual_specs(tm, k, ms, tn, layer, col // tn, nt),
        out_specs=[blk_p] * len(dts) + [blk_s] * len(dts),
        out_shape=([jax.ShapeDtypeStruct((mp, tn), dt) for dt in dts]
                   + [jax.ShapeDtypeStruct((ms, tn), dt) for dt in dts]),
        scratch_shapes=[pltpu.VMEM((tn, k), BF16)],
        compiler_params=_cparams(("parallel", "arbitrary")),
        name=name,
    )(xp, xs, w_in_t, *tabs_p, *tabs_s)


def _emit_kiwi(p, cos, sin, of_ref, ob_ref, rows):
    r = _rope(p, cos, sin, IDX_DIM // 2)
    lane = lax.broadcasted_iota(I32, p.shape, 1)
    of_ref[rows, :] = jnp.where(lane < IDX_DIM, r,
                                jnp.where(lane < IDX_DIM + IDX_HEADS, p * WI_SCALE, 0.0))
    ob_ref[rows, :] = jnp.where(lane < IDX_DIM, r, pltpu.roll(r, IDX_DIM, 1)).astype(BF16)


def _proj_kiwi_kernel(xp_ref, xs_ref, w_ref, cp_ref, sp_ref, cs_ref, ss_ref,
                      ofp_ref, obp_ref, ofs_ref, obs_ref, wb_ref):
    @pl.when(pl.program_id(1) == 0)
    def _():
        _stage_weight(w_ref, wb_ref)
        _emit_kiwi(_dot_nt(xs_ref[...], wb_ref[...]), cs_ref[...], ss_ref[...], ofs_ref, obs_ref,
                   slice(None))

    for rows in _row_blocks(xp_ref.shape[0]):
        _emit_kiwi(_dot_nt(xp_ref[rows, :], wb_ref[...]), cp_ref[rows, :], sp_ref[rows, :],
                   ofp_ref, obp_ref, rows)


def _proj_kiwi(xp, xs, w_kiwi_t, tabs_p, tabs_s, layer, tm):
    mp, k = xp.shape
    ms = xs.shape[0]
    nt = tabs_p[0].shape[0] // tm
    blk_p = pl.BlockSpec((tm, KIWI_W), lambda j, i: (i, 0))
    blk_s = pl.BlockSpec((ms, KIWI_W), lambda j, i: (0, 0))
    sds = lambda m, dt: jax.ShapeDtypeStruct((m, KIWI_W), dt)
    return pl.pallas_call(
        _proj_kiwi_kernel,
        grid=(1, mp // tm),
        in_specs=_dual_specs(tm, k, ms, KIWI_W, layer, 0, nt),
        out_specs=[blk_p, blk_p, blk_s, blk_s],
        out_shape=[sds(mp, F32), sds(mp, BF16), sds(ms, F32), sds(ms, BF16)],
        scratch_shapes=[pltpu.VMEM((KIWI_W, k), BF16)],
        compiler_params=_cparams(("parallel", "arbitrary")),
        name="proj_kiwi",
    )(xp, xs, w_kiwi_t, *tabs_p, *tabs_s)


def _gelu_tanh(x):
    return 0.5 * x * (1.0 + jnp.tanh(0.7978845608028654 * (x + 0.044715 * (x * x * x))))


def _lru_coeffs(xc, ga, gx, gab, gxb, lam):
    r = _sigmoid(ga + gab)
    i = _sigmoid(gx + gxb)
    z = -lam
    softplus = jnp.maximum(z, 0.0) + jnp.log(1.0 + jnp.exp(-jnp.abs(z)))
    log_a = (-LRU_C * softplus) * r
    a = jnp.exp(log_a)
    b = jnp.sqrt(1.0 - a * a) * (i * xc)
    return a, b


def _lru_prompt_kernel(xb_ref, gt_ref, cw_ref, cb_ref, gw_ref, gab_ref, gxb_ref, lam_ref,
                       y_ref, hl_ref, cl_ref, xext_ref, h_ref, a_ref, b_ref, *, tc):
    t = pl.program_id(2)

    @pl.when(t == 0)
    def _():
        xext_ref[0:8, :] = jnp.zeros((8, LRU_BW), F32)
        h_ref[...] = jnp.zeros_like(h_ref)

    x = xb_ref[...]
    xext_ref[8:8 + tc, :] = x
    cw = cw_ref[...]
    xc = (cb_ref[...] + cw[3:4] * x + cw[2:3] * xext_ref[7:7 + tc, :]
          + cw[1:2] * xext_ref[6:6 + tc, :] + cw[0:1] * xext_ref[5:5 + tc, :])
    cl_ref[0] = xext_ref[tc + 5:tc + 8, :]
    xext_ref[0:8, :] = x[tc - 8:tc, :]

    g = _dot(xc.astype(BF16), gw_ref[0])
    a, b = _lru_coeffs(xc, g[:, :LRU_BW], g[:, LRU_BW:], gab_ref[...], gxb_ref[...], lam_ref[...])

    def doubling(a, b, pos, n):
        d = 1
        while d < n:
            live = pos >= d
            b = jnp.where(live, a * pltpu.roll(b, d, 0) + b, b)
            a = jnp.where(live, a * pltpu.roll(a, d, 0), a)
            d *= 2
        return a, b

    ng = tc // 8
    row = lax.broadcasted_iota(I32, (tc, LRU_BW), 0)
    a, b = doubling(a, b, row & 7, 8)
    a_ref[...] = a
    b_ref[...] = b
    grow = lax.broadcasted_iota(I32, (ng, LRU_BW), 0)
    ga, gb = doubling(a_ref[pl.ds(7, ng, stride=8), :], b_ref[pl.ds(7, ng, stride=8), :], grow, ng)
    h_group = ga * h_ref[...] + gb
    h_in = jnp.where(grow == 0, h_ref[...], pltpu.roll(h_group, 1, 0))
    h = jnp.concatenate([a[8 * i:8 * i + 8, :] * h_in[i:i + 1, :] + b[8 * i:8 * i + 8, :]
                         for i in range(ng)], axis=0)
    h_ref[...] = h_group[ng - 1:ng, :]
    hl_ref[0] = h_group[ng - 1:ng, :]
    y_ref[...] = (h * _gelu_tanh(gt_ref[...])).astype(BF16)


def _lru_prompt(xg, conv_w, conv_b, gw, gab, gxb, lam, layer, batch, seq, tc):
    nt = seq // tc
    vec = pl.BlockSpec((1, LRU_BW), lambda b, n, t: (0, n))
    y, h_last, conv_last = pl.pallas_call(
        functools.partial(_lru_prompt_kernel, tc=tc),
        grid=(batch, LRU_BLOCKS, nt),
        in_specs=[pl.BlockSpec((tc, LRU_BW), lambda b, n, t: (b * nt + t, n)),
                  pl.BlockSpec((tc, LRU_BW), lambda b, n, t: (b * nt + t, LRU_BLOCKS + n)),
                  pl.BlockSpec((CONV_W, LRU_BW), lambda b, n, t: (0, n)),
                  vec,
                  pl.BlockSpec((None, 1, LRU_BW, 2 * LRU_BW), lambda b, n, t: (layer, n, 0, 0)),
                  vec, vec, vec],
        out_specs=[pl.BlockSpec((tc, LRU_BW), lambda b, n, t: (b * nt + t, n)),
                   pl.BlockSpec((1, 1, LRU_BW), lambda b, n, t: (b, 0, n)),
                   pl.BlockSpec((1, CONV_W - 1, LRU_BW), lambda b, n, t: (b, 0, n))],
        out_shape=[jax.ShapeDtypeStruct((batch * seq, D_LRU), BF16),
                   jax.ShapeDtypeStruct((batch, 1, D_LRU), F32),
                   jax.ShapeDtypeStruct((batch, CONV_W - 1, D_LRU), F32)],
        scratch_shapes=[pltpu.VMEM((tc + 8, LRU_BW), F32), pltpu.VMEM((1, LRU_BW), F32),
                        pltpu.VMEM((tc, LRU_BW), F32), pltpu.VMEM((tc, LRU_BW), F32)],
        compiler_params=_cparams(("parallel", "parallel", "arbitrary")),
        name="rglru_prompt",
    )(xg, xg, conv_w, conv_b, gw, gab, gxb, lam)
    return y, h_last[:, 0], conv_last


def _lru_sample_kernel(xg_ref, c0_ref, c1_ref, c2_ref, h0_ref, cw_ref, cb_ref, gw_ref,
                       gab_ref, gxb_ref, lam_ref, y_ref, h_ref):
    xb = xg_ref[:, :D_LRU]
    gate = xg_ref[:, D_LRU:]
    cw = cw_ref[...]
    xc = (cb_ref[...] + cw[0:1] * c0_ref[...] + cw[1:2] * c1_ref[...]
          + cw[2:3] * c2_ref[...] + cw[3:4] * xb)
    xcb = xc.astype(BF16)
    ga, gx = [], []
    for n in range(LRU_BLOCKS):
        g = _dot(xcb[:, n * LRU_BW:(n + 1) * LRU_BW], gw_ref[n])
        ga.append(g[:, :LRU_BW])
        gx.append(g[:, LRU_BW:])
    a, b = _lru_coeffs(xc, jnp.concatenate(ga, -1), jnp.concatenate(gx, -1),
                       gab_ref[...], gxb_ref[...], lam_ref[...])
    h = a * h0_ref[...] + b
    h_ref[...] = h
    y_ref[...] = (h * _gelu_tanh(gate)).astype(BF16)


def _lru_sample(xg, conv_state, h0, conv_w, conv_b, gw_l, gab, gxb, lam):
    rows = xg.shape[0]
    return pl.pallas_call(
        _lru_sample_kernel,
        out_shape=[jax.ShapeDtypeStruct((rows, D_LRU), BF16), jax.ShapeDtypeStruct((rows, D_LRU), F32)],
        compiler_params=pltpu.CompilerParams(vmem_limit_bytes=VMEM_LIMIT),
        name="rglru_sample",
    )(xg, conv_state[:, 0], conv_state[:, 1], conv_state[:, 2], h0, conv_w, conv_b, gw_l, gab, gxb, lam)


def _sortable_key(s):
    s = jnp.where(s == 0.0, 0.0, s)
    bits = pltpu.bitcast(s, I32)
    return bits ^ ((bits >> 31) & 0x7FFFFFFF)


def _bit_transpose32(words):
    a = list(words)
    j = 16
    m = 0x0000FFFF
    while j:
        k = 0
        while k < 32:
            t = (a[k] ^ lax.shift_right_logical(a[k + j], j)) & m
            a[k] = a[k] ^ t
            a[k + j] = a[k + j] ^ jnp.left_shift(t, j)
            k = (k + j + 1) & ~j
        j >>= 1
        m = (m ^ (m << j)) & 0xFFFFFFFF
    return a


def _attn_prompt_kernel(q_ref, qi_ref, wt_ref, k_ref, vt_ref, kd_ref, o_ref,
                        keys_ref, planes_ref, cand_ref, bias_ref, qm_ref, acc_ref, ml_ref, *, tq):
    i = pl.program_id(1)
    nk = i + 1
    ck = tq

    lane = lax.broadcasted_iota(I32, (tq, 128), 1)
    for p in range(IDX_HEADS // 2):
        slab = qi_ref[:, p * 128:(p + 1) * 128]
        zero = jnp.zeros_like(slab)
        qm_ref[2 * p] = jnp.where(lane < IDX_DIM, slab, zero)
        qm_ref[2 * p + 1] = jnp.where(lane >= IDX_DIM, slab, zero)

    qpos = i * tq + lax.broadcasted_iota(I32, (1, tq), 1)
    kidx = lax.broadcasted_iota(I32, (ck, tq), 0)

    def score_body(c, carry):
        off = pl.multiple_of(c * ck, ck)
        kd = kd_ref[0, pl.ds(off, ck), :]
        acc = jnp.zeros((ck, tq), F32)
        for h in range(IDX_HEADS):
            x = _dot_nt(kd, qm_ref[h])
            acc = acc + wt_ref[0, h:h + 1, :] * jnp.maximum(x, 0.0)
        key = jnp.where(kidx + off <= qpos, _sortable_key(acc), INT_MIN)
        keys_ref[pl.ds(off, ck), :] = key
        ukey = key ^ INT_MIN
        planes = _bit_transpose32([ukey[r * 8:(r + 1) * 8, :] for r in range(32)])
        for b in range(32):
            planes_ref[c, b] = planes[b]
        cand_ref[c] = jnp.full((8, tq), -1, I32)
        return carry

    lax.fori_loop(0, nk, score_body, 0)

    kk = jnp.minimum(qpos + 1, TOPK)

    def sweep(b, drop, first):
        def body(c, acc):
            cand = cand_ref[c]
            if not first:
                cand = cand & (planes_ref[c, b - 1] ^ drop)
                cand_ref[c] = cand
            return acc + lax.population_count(cand & planes_ref[c, b])
        acc = lax.fori_loop(0, nk, body, jnp.zeros((8, tq), I32))
        return acc.sum(axis=0, keepdims=True)

    def decide(b, n_ones, rem, t_u):
        take = n_ones >= rem
        t_u = jnp.where(take, t_u | jnp.left_shift(1, 31 - b), t_u)
        rem = jnp.where(take, rem, rem - n_ones)
        return rem, t_u, jnp.where(take, 0, -1)

    rem, t_u, drop = decide(0, sweep(0, None, True), kk, jnp.zeros((1, tq), I32))

    def bit_body(b, carry):
        rem, t_u, drop = carry
        return decide(b, sweep(b, drop, False), rem, t_u)

    rem, t_u, drop = lax.fori_loop(1, 32, bit_body, (rem, t_u, drop))

    def last_body(c, acc):
        return acc + lax.population_count(cand_ref[c] & (planes_ref[c, 31] ^ drop))

    n_eq = lax.fori_loop(0, nk, last_body, jnp.zeros((8, tq), I32)).sum(axis=0, keepdims=True)
    t = t_u ^ INT_MIN
    need = rem

    def count(pred):
        def body(c, acc):
            off = pl.multiple_of(c * ck, ck)
            m = jnp.where(pred(keys_ref[pl.ds(off, ck), :], off), 1, 0)
            return acc + m.reshape(ck // 8, 8, tq).sum(axis=0)
        acc = lax.fori_loop(0, nk, body, jnp.zeros((8, tq), I32))
        return acc.sum(axis=0, keepdims=True)

    tie = jnp.max(n_eq - need) > 0
    jbits = 13

    def j_body(it, jcut):
        j_try = jcut + jnp.left_shift(1, jbits - 1 - it)
        n = count(lambda kc, off: (kc == t) & (kidx + off < j_try))
        return jnp.where(n <= need, j_try, jcut)

    jcut = lax.fori_loop(0, jnp.where(tie, jbits, 0), j_body,
                         jnp.where(tie, 0, 2 ** jbits) + jnp.zeros((1, tq), I32))

    def bias_body(c, carry):
        off = pl.multiple_of(c * ck, ck)
        kc = keys_ref[pl.ds(off, ck), :]
        sel = (kc > t) | ((kc == t) & (kidx + off < jcut))
        bias_ref[pl.ds(off, ck), :] = jnp.where(sel, 0.0, NEG_BIG)
        return carry

    lax.fori_loop(0, nk, bias_body, 0)

    ml_ref[0:N_HEADS, :] = jnp.full((N_HEADS, tq), NEG_BIG, F32)
    ml_ref[N_HEADS:, :] = jnp.zeros((N_HEADS, tq), F32)
    acc_ref[...] = jnp.zeros_like(acc_ref)

    def att_body(c, carry):
        for j in range(ck // ATT_SUB):
            off = pl.multiple_of(c * ck + j * ATT_SUB, ATT_SUB)
            bias = bias_ref[pl.ds(off, ATT_SUB), :]
            for h in range(N_HEADS):
                hs = slice(h * HEAD_DIM, (h + 1) * HEAD_DIM)
                s = _dot_nt(k_ref[0, pl.ds(off, ATT_SUB), hs], q_ref[:, hs]) + bias
                m_old = ml_ref[h:h + 1, :]
                m_new = jnp.maximum(m_old, s.max(axis=0, keepdims=True))
                alpha = jnp.exp2(m_old - m_new)
                p = jnp.exp2(s - m_new)
                ml_ref[h:h + 1, :] = m_new
                ml_ref[N_HEADS + h:N_HEADS + h + 1, :] = (
                    ml_ref[N_HEADS + h:N_HEADS + h + 1, :] * alpha + p.sum(axis=0, keepdims=True))
                pv = _dot(vt_ref[0, hs, pl.ds(off, ATT_SUB)], p.astype(BF16))
                acc_ref[h] = acc_ref[h] * alpha + pv
        return carry

    lax.fori_loop(0, nk, att_body, 0)

    for h in range(N_HEADS):
        out = acc_ref[h] / ml_ref[N_HEADS + h:N_HEADS + h + 1, :]
        o_ref[:, h * HEAD_DIM:(h + 1) * HEAD_DIM] = out.T.astype(BF16)


def _attn_prompt(q, qi, wt, k, vt, kd, batch, seq, tq):
    nq = seq // tq
    rows = lambda b, i: (b * nq + i, 0)
    whole = lambda b, i: (b, 0, 0)
    return pl.pallas_call(
        functools.partial(_attn_prompt_kernel, tq=tq),
        grid=(batch, nq),
        in_specs=[pl.BlockSpec((tq, D_ATT), rows),
                  pl.BlockSpec((tq, D_QIDX), rows),
                  pl.BlockSpec((1, IDX_HEADS, tq), lambda b, i: (b, 0, i)),
                  pl.BlockSpec((1, seq, D_ATT), whole),
                  pl.BlockSpec((1, D_ATT, seq), whole),
                  pl.BlockSpec((1, seq, KIWI_W), whole)],
        out_specs=pl.BlockSpec((tq, D_ATT), rows),
        out_shape=jax.ShapeDtypeStruct((batch * seq, D_ATT), BF16),
        scratch_shapes=[pltpu.VMEM((seq, tq), I32),
                        pltpu.VMEM((nq, 32, 8, tq), I32),
                        pltpu.VMEM((nq, 8, tq), I32),
                        pltpu.VMEM((seq, tq), F32),
                        pltpu.VMEM((IDX_HEADS, tq, 128), BF16),
                        pltpu.VMEM((N_HEADS, HEAD_DIM, tq), F32),
                        pltpu.VMEM((2 * N_HEADS, tq), F32)],
        compiler_params=_cparams(("parallel", "arbitrary")),
        name="dsa_prompt",
    )(q, qi, wt, k, vt, kd)


def _select_sample_kernel(pt_ref, qi_ref, w_ref, kn_ref, cache_ref, idx_ref,
                          buf_ref, sem, sc_ref, *, layer, n_pages):
    b = pl.program_id(0)
    past = n_pages * PAGE

    def page_copy(p):
        return pltpu.make_async_copy(cache_ref.at[layer, pt_ref[b, p]], buf_ref.at[p], sem)

    def start(p, c):
        page_copy(p).start()
        return c

    lax.fori_loop(0, n_pages, start, 0)

    def wait(p, c):
        page_copy(p).wait()
        return c

    lax.fori_loop(0, n_pages, wait, 0)

    qi = qi_ref[0]
    w = w_ref[0]
    group = 8

    def score_body(g, c):
        for j in range(group):
            p = g * group + j
            x = _dot(qi, buf_ref[p].astype(BF16))
            sc_ref[pl.ds(p, 1), :] = (w * jnp.maximum(x, 0.0)).sum(axis=0, keepdims=True)
        return c

    lax.fori_loop(0, n_pages // group, score_body, 0)

    kn = kn_ref[0].astype(BF16).astype(F32)
    x_new = (qi.astype(F32) * kn).sum(axis=1, keepdims=True)
    s_new = (w * jnp.maximum(x_new, 0.0)).sum(axis=0, keepdims=True)

    keys = _sortable_key(sc_ref[...])
    key_new = _sortable_key(s_new)
    pos = (lax.broadcasted_iota(I32, (n_pages, PAGE), 0) * PAGE
           + lax.broadcasted_iota(I32, (n_pages, PAGE), 1))

    def count(m_past, m_new):
        return jnp.sum(jnp.where(m_past, 1, 0)) + jnp.sum(jnp.where(m_new, 1, 0))

    t0 = jnp.where(count(keys >= 0, key_new >= 0) >= TOPK, 0, INT_MIN)

    def bit_body(it, t):
        t_try = t | jnp.left_shift(1, 30 - it)
        return jnp.where(count(keys >= t_try, key_new >= t_try) >= TOPK, t_try, t)

    t = lax.fori_loop(0, 31, bit_body, t0)
    need = TOPK - count(keys > t, key_new > t)
    tie = count(keys == t, key_new == t) > need
    jbits = 15

    def j_body(it, jcut):
        j_try = jcut + jnp.left_shift(1, jbits - 1 - it)
        n = count((keys == t) & (pos < j_try), (key_new == t) & (past < j_try))
        return jnp.where(n <= need, j_try, jcut)

    jcut = lax.fori_loop(0, jnp.where(tie, jbits, 0), j_body, jnp.where(tie, 0, 2 ** jbits))
    sel = (keys > t) | ((keys == t) & (pos < jcut))
    n_past = jnp.sum(jnp.where(sel, 1, 0))

    selb = jnp.where(sel, 1.0, 0.0).astype(BF16)
    r_i = lax.broadcasted_iota(I32, (PAGE, PAGE), 0)
    c_i = lax.broadcasted_iota(I32, (PAGE, PAGE), 1)
    incl = _dot(selb, jnp.where(r_i <= c_i, 1.0, 0.0).astype(BF16))
    cnt = jnp.broadcast_to(incl[:, PAGE - 1:PAGE], (n_pages, TOPK))
    pr_i = lax.broadcasted_iota(I32, (n_pages, n_pages), 0)
    pc_i = lax.broadcasted_iota(I32, (n_pages, n_pages), 1)
    pinc = _dot(jnp.where(pc_i <= pr_i, 1.0, 0.0).astype(BF16), cnt.astype(BF16))
    slot = lax.broadcasted_iota(I32, (n_pages, TOPK), 1).astype(F32)
    before = pinc <= slot
    page = jnp.sum(jnp.where(before, 1, 0), axis=0, keepdims=True)
    pexc = jnp.sum(jnp.where(before, cnt, 0.0), axis=0, keepdims=True)
    rank = slot[0:1, :] - pexc
    onehot = jnp.where(lax.broadcasted_iota(I32, (n_pages, TOPK), 0) == page, 1.0, 0.0).astype(BF16)
    incl_sel = _dot(incl.T.astype(BF16), onehot)
    off = jnp.sum(jnp.where(incl_sel <= rank, 1, 0), axis=0, keepdims=True)
    slot_i = lax.broadcasted_iota(I32, (1, TOPK), 1)
    idx_ref[0] = jnp.where(slot_i < n_past, page * PAGE + off, past)


def _select_sample(page_table, qi, w, ki_new, cache_kidx_t, layer):
    bd, n_pages = page_table.shape
    grid_spec = pltpu.PrefetchScalarGridSpec(
        num_scalar_prefetch=1,
        grid=(bd,),
        in_specs=[pl.BlockSpec((1, IDX_HEADS, IDX_DIM), lambda b, pt: (b, 0, 0)),
                  pl.BlockSpec((1, IDX_HEADS, 1), lambda b, pt: (b, 0, 0)),
                  pl.BlockSpec((1, 1, IDX_DIM), lambda b, pt: (b, 0, 0)),
                  pl.BlockSpec(memory_space=pl.ANY)],
        out_specs=pl.BlockSpec((1, 1, TOPK), lambda b, pt: (b, 0, 0)),
        scratch_shapes=[pltpu.VMEM((n_pages, IDX_DIM, PAGE), F32),
                        pltpu.SemaphoreType.DMA(()),
                        pltpu.VMEM((n_pages, PAGE), F32)],
    )
    return pl.pallas_call(
        functools.partial(_select_sample_kernel, layer=layer, n_pages=n_pages),
        grid_spec=grid_spec,
        out_shape=jax.ShapeDtypeStruct((bd, 1, TOPK), I32),
        compiler_params=_cparams(("arbitrary",)),
        name="dsa_sample_select",
    )(page_table, qi, w, ki_new, cache_kidx_t)


def _attn_sample_kernel(pt_ref, idx_ref, q_ref, kn_ref, vn_ref, ck_ref, cv_ref, o_ref,
                        kbuf_ref, vbuf_ref, sems, *, layer, n_pages):
    b = pl.program_id(0)
    past = n_pages * PAGE

    def row_copies(r):
        s = jnp.minimum(idx_ref[b, r], past - 1)
        page = pt_ref[b, s // PAGE]
        off = s % PAGE
        return (pltpu.make_async_copy(ck_ref.at[layer, page, off], kbuf_ref.at[r], sems.at[0]),
                pltpu.make_async_copy(cv_ref.at[layer, page, off], vbuf_ref.at[r], sems.at[1]))

    def start(r, c):
        ck, cv = row_copies(r)
        ck.start()
        cv.start()
        return c

    lax.fori_loop(0, TOPK, start, 0, unroll=8)

    def wait(r, c):
        ck, cv = row_copies(r)
        ck.wait()
        cv.wait()
        return c

    lax.fori_loop(0, TOPK, wait, 0)

    @pl.when(idx_ref[b, TOPK - 1] == past)
    def _():
        kbuf_ref[TOPK - 1] = kn_ref[0]
        vbuf_ref[TOPK - 1] = vn_ref[0]

    q = q_ref[...]
    logits = (kbuf_ref[...] * q).sum(axis=-1, keepdims=True)
    m = logits.max(axis=0, keepdims=True)
    p = jnp.exp2(logits - m)
    den = p.sum(axis=0, keepdims=True)
    o_ref[...] = (p * vbuf_ref[...]).sum(axis=0, keepdims=True) / den


def _attn_sample(page_table, idx, q, k_new, v_new, cache_k, cache_v, layer):
    bd, n_pages = page_table.shape
    row = pl.BlockSpec((1, N_HEADS, HEAD_DIM), lambda b, pt, ix: (b, 0, 0))
    grid_spec = pltpu.PrefetchScalarGridSpec(
        num_scalar_prefetch=2,
        grid=(bd,),
        in_specs=[row, row, row, pl.BlockSpec(memory_space=pl.ANY), pl.BlockSpec(memory_space=pl.ANY)],
        out_specs=row,
        scratch_shapes=[pltpu.VMEM((TOPK, N_HEADS, HEAD_DIM), F32),
                        pltpu.VMEM((TOPK, N_HEADS, HEAD_DIM), F32),
                        pltpu.SemaphoreType.DMA((2,))],
    )
    return pl.pallas_call(
        functools.partial(_attn_sample_kernel, layer=layer, n_pages=n_pages),
        grid_spec=grid_spec,
        out_shape=jax.ShapeDtypeStruct((bd, N_HEADS, HEAD_DIM), F32),
        compiler_params=_cparams(("arbitrary",)),
        name="dsa_sample_attend",
    )(page_table, idx, q, k_new, v_new, cache_k, cache_v)


def _layer_norm(z, g, b):
    mu = jnp.mean(z, axis=-1, keepdims=True)
    zc = z - mu
    var = jnp.mean(zc * zc, axis=-1, keepdims=True)
    return zc * lax.rsqrt(var + LN_EPS) * g + b


def _mix_ln_kernel(yl_ref, ya_ref, w_ref, x_ref, g_ref, b_ref, of_ref, ob_ref, *, alpha):
    mix = _dot(yl_ref[...], w_ref[:D_LRU, :]) + _dot(ya_ref[...], w_ref[D_LRU:, :])
    y = _layer_norm(alpha * x_ref[...] + mix, g_ref[...], b_ref[...])
    of_ref[...] = y
    ob_ref[...] = y.astype(BF16)


def _mix_ln(y_lru, y_att, w_out, x, g, b, layer, tm, alpha):
    m = x.shape[0]
    row = lambda i: (i, 0)
    fixed = lambda i: (0, 0)
    return pl.pallas_call(
        functools.partial(_mix_ln_kernel, alpha=alpha),
        grid=(m // tm,),
        in_specs=[pl.BlockSpec((tm, D_LRU), row), pl.BlockSpec((tm, D_ATT), row),
                  pl.BlockSpec((None, D_LRU + D_ATT, D_MODEL), lambda i: (layer, 0, 0)),
                  pl.BlockSpec((tm, D_MODEL), row),
                  pl.BlockSpec((1, D_MODEL), fixed), pl.BlockSpec((1, D_MODEL), fixed)],
        out_specs=[pl.BlockSpec((tm, D_MODEL), row), pl.BlockSpec((tm, D_MODEL), row)],
        out_shape=[jax.ShapeDtypeStruct((m, D_MODEL), F32), jax.ShapeDtypeStruct((m, D_MODEL), BF16)],
        compiler_params=_cparams(("parallel",)),
        name="mix_ln1",
    )(y_lru, y_att, w_out, x, g, b)


def _ffn_up_kernel(xp_ref, xs_ref, wg_ref, wu_ref, op_ref, os_ref, wgb_ref, wub_ref):
    def act(x):
        g = _dot(x, wgb_ref[...])
        u = _dot(x, wub_ref[...])
        return (g * _sigmoid(g) * u).astype(BF16)

    @pl.when(pl.program_id(1) == 0)
    def _():
        _stage_weight(wg_ref, wgb_ref)
        _stage_weight(wu_ref, wub_ref)
        os_ref[...] = act(xs_ref[...])

    tm = xp_ref.shape[0]
    for r in range(0, tm, ROW_BLOCK):
        op_ref[r:r + ROW_BLOCK, :] = act(xp_ref[r:r + ROW_BLOCK, :])


def _ffn_up(xp, xs, wg, wu, layer, tm, tn):
    mp, k = xp.shape
    ms = xs.shape[0]
    wspec = pl.BlockSpec((None, k, tn), lambda j, i: (layer, 0, j))
    return pl.pallas_call(
        _ffn_up_kernel,
        grid=(D_FF // tn, mp // tm),
        in_specs=[pl.BlockSpec((tm, k), lambda j, i: (i, 0)),
                  pl.BlockSpec((ms, k), lambda j, i: (0, 0)),
                  wspec, wspec],
        out_specs=[pl.BlockSpec((tm, tn), lambda j, i: (i, j)),
                   pl.BlockSpec((ms, tn), lambda j, i: (0, j))],
        out_shape=[jax.ShapeDtypeStruct((mp, D_FF), BF16), jax.ShapeDtypeStruct((ms, D_FF), BF16)],
        scratch_shapes=[pltpu.VMEM((k, tn), BF16), pltpu.VMEM((k, tn), BF16)],
        compiler_params=_cparams(("parallel", "arbitrary")),
        name="ffn_up",
    )(xp, xs, wg, wu)


def _ffn_down_ln_kernel(h_ref, w_ref, x_ref, g_ref, b_ref, of_ref, ob_ref, *, alpha):
    k = pl.program_id(1)
    last = pl.num_programs(1) - 1
    tm, tk = h_ref.shape
    blocks = [slice(r, r + min(tm, ROW_BLOCK)) for r in range(0, tm, ROW_BLOCK)]

    def part(rows):
        return _dot(h_ref[rows, :], w_ref[pl.ds(pl.multiple_of(k * tk, tk), tk), :])

    @pl.when(k == 0)
    def _():
        for rows in blocks:
            of_ref[rows, :] = part(rows)

    @pl.when((k > 0) & (k < last))
    def _():
        for rows in blocks:
            of_ref[rows, :] += part(rows)

    @pl.when(k == last)
    def _():
        for rows in blocks:
            z = alpha * x_ref[rows, :] + (of_ref[rows, :] + part(rows))
            y = _layer_norm(z, g_ref[...], b_ref[...])
            of_ref[rows, :] = y
            ob_ref[rows, :] = y.astype(BF16)


def _ffn_down_ln(h, wd, x, g, b, layer, tm, tk, alpha):
    m = x.shape[0]
    row = lambda i, k: (i, 0)
    fixed = lambda i, k: (0, 0)
    assert D_FF // tk >= 2
    return pl.pallas_call(
        functools.partial(_ffn_down_ln_kernel, alpha=alpha),
        grid=(m // tm, D_FF // tk),
        in_specs=[pl.BlockSpec((tm, tk), lambda i, k: (i, k)),
                  pl.BlockSpec((None, D_FF, D_MODEL), lambda i, k: (layer, 0, 0),
                               pipeline_mode=pl.Buffered(1)),
                  pl.BlockSpec((tm, D_MODEL), row),
                  pl.BlockSpec((1, D_MODEL), fixed), pl.BlockSpec((1, D_MODEL), fixed)],
        out_specs=[pl.BlockSpec((tm, D_MODEL), row), pl.BlockSpec((tm, D_MODEL), row)],
        out_shape=[jax.ShapeDtypeStruct((m, D_MODEL), F32), jax.ShapeDtypeStruct((m, D_MODEL), BF16)],
        compiler_params=_cparams(("parallel", "arbitrary")),
        name="ffn_down_ln2",
    )(h, wd, x, g, b)


def _tiles(rows):
    if rows <= SAMPLE_ROWS:
        return dict(mix=rows, down=rows)
    return dict(proj=1024, mix=512, up=1024, down=512)


def _project(xp16, xs16, w, layer, tabs_p, tabs_s):
    tm = _tiles(xp16.shape[0])["proj"]
    heads = functools.partial(_proj_heads, xp16, xs16, w["in_t"], layer=layer, tm=tm)
    t128 = (tabs_p[:2], tabs_s[:2])
    t64 = (tabs_p[2:], tabs_s[2:])
    out = {}
    out["xg"] = _proj_lru(xp16, xs16, w["in_t"], layer, tm, 1024)
    out["q"] = heads(*t128, col=O_QKV, half=HEAD_DIM // 2, scale=Q_SCALE, keep_f32=False, name="proj_q")
    kfp, kbp, kfs, kbs = heads(*t128, col=O_QKV + D_ATT, half=HEAD_DIM // 2, scale=1.0,
                               keep_f32=True, name="proj_k")
    vfp, vbp, vfs, vbs = heads(*t128, col=O_QKV + 2 * D_ATT, half=0, scale=1.0,
                               keep_f32=True, name="proj_v")
    out["kf"], out["kb"], out["vf"], out["vb"] = (kfp, kfs), (kbp, kbs), (vfp, vfs), (vbp, vbs)
    out["qi"] = heads(*t64, col=O_QI, half=IDX_DIM // 2, scale=1.0, keep_f32=False, name="proj_qi")
    fp, bp_, fs, bs = _proj_kiwi(xp16, xs16, w["kiwi_t"], *t64, layer, tm)
    out["kiwi_f"], out["kiwi_b"] = (fp, fs), (bp_, bs)
    return out


def _finish(xs, ys_lru, ys_att, w, ln, layer, alpha):
    x1, x1b = [], []
    for x, y_lru, y_att in zip(xs, ys_lru, ys_att):
        a, b = _mix_ln(y_lru, y_att, w["out"], x, ln[0], ln[1], layer, _tiles(x.shape[0])["mix"], alpha)
        x1.append(a)
        x1b.append(b)
    hs = _ffn_up(x1b[0], x1b[1], w["ffn_gate"], w["ffn_up"], layer, _tiles(xs[0].shape[0])["up"], 512)
    return [_ffn_down_ln(h, w["ffn_down"], x, ln[2], ln[3], layer, _tiles(x.shape[0])["down"],
                         D_FF // 4, alpha) for h, x in zip(hs, x1)]


def kernel(x_prompt, x_sample, cache_k, cache_v, cache_kidx, state_lru_h, state_lru_conv, page_table,
           w_in, conv_w, conv_b, gate_a_w, gate_a_b, gate_x_w, gate_x_b, lru_lambda, w_out,
           ln1_g, ln1_b, w_ffn_gate, w_ffn_up, w_ffn_down, ln2_g, ln2_b):
    depth = w_in.shape[0]
    bp, tp, _ = x_prompt.shape
    bd = x_sample.shape[0]
    n_pages = page_table.shape[1]
    past = n_pages * PAGE
    alpha = (2.0 * depth) ** 0.25
    pad = SAMPLE_ROWS - bd

    pos_p = jnp.arange(tp, dtype=I32)
    pos_s = jnp.full((SAMPLE_ROWS,), past, I32)
    tabs_p = _rope_tables(pos_p, HEAD_DIM) + _rope_tables(pos_p, IDX_DIM)
    tabs_s = _rope_tables(pos_s, HEAD_DIM) + _rope_tables(pos_s, IDX_DIM)

    xp = x_prompt.reshape(bp * tp, D_MODEL)
    xs = jnp.pad(x_sample.reshape(bd, D_MODEL), ((0, pad), (0, 0)))
    xp16 = xp.astype(BF16)
    xs16 = xs.astype(BF16)

    w_in_t = jnp.swapaxes(w_in, 1, 2)
    n_kiwi = w_in.shape[2] - O_KIWI
    w = {
        "in_t": w_in_t,
        "kiwi_t": jnp.pad(w_in_t[:, O_KIWI:, :], ((0, 0), (0, KIWI_W - n_kiwi), (0, 0))),
        "out": w_out.astype(BF16),
        "ffn_gate": w_ffn_gate,
        "ffn_up": w_ffn_up,
        "ffn_down": w_ffn_down.astype(BF16),
    }
    gw = jnp.concatenate([gate_a_w, gate_x_w], axis=-1).astype(BF16)
    cache_kidx_t = jnp.swapaxes(cache_kidx, 2, 3)

    outs = {n: [] for n in ("kp", "vp", "kip", "hp", "cp", "ks", "vs", "kis", "hs", "cs")}
    for l in range(depth):
        ln = (ln1_g[l][None], ln1_b[l][None], ln2_g[l][None], ln2_b[l][None])
        lru_vecs = (conv_w[l], conv_b[l][None])
        gate_vecs = (gate_a_b[l][None], gate_x_b[l][None], lru_lambda[l][None])
        pr = _project(xp16, xs16, w, l, tabs_p, tabs_s)

        kiwi_f = pr["kiwi_f"][0]
        yp_lru, h_last, conv_last = _lru_prompt(pr["xg"][0], *lru_vecs, gw, *gate_vecs, layer=l,
                                                batch=bp, seq=tp, tc=512)
        wt = jnp.swapaxes(kiwi_f.reshape(bp, tp, KIWI_W)[:, :, IDX_DIM:IDX_DIM + IDX_HEADS], 1, 2)
        vt = jnp.swapaxes(pr["vb"][0].reshape(bp, tp, D_ATT), 1, 2)
        yp_att = _attn_prompt(pr["q"][0], pr["qi"][0], wt, pr["kb"][0].reshape(bp, tp, D_ATT), vt,
                              pr["kiwi_b"][0].reshape(bp, tp, KIWI_W), batch=bp, seq=tp, tq=256)
        outs["kp"].append(pr["kf"][0].reshape(bp, tp, N_HEADS, HEAD_DIM))
        outs["vp"].append(pr["vf"][0].reshape(bp, tp, N_HEADS, HEAD_DIM))
        outs["kip"].append(kiwi_f[:, :IDX_DIM].reshape(bp, tp, IDX_DIM))
        outs["hp"].append(h_last)
        outs["cp"].append(conv_last)

        xg, kiwi_f = pr["xg"][1], pr["kiwi_f"][1]
        conv_state = jnp.pad(state_lru_conv[l], ((0, pad), (0, 0), (0, 0)))
        h0 = jnp.pad(state_lru_h[l], ((0, pad), (0, 0)))
        ys_lru, h_new = _lru_sample(xg, conv_state, h0, *lru_vecs, gw[l], *gate_vecs)
        ki_new = kiwi_f[:bd, :IDX_DIM]
        idx = _select_sample(page_table, pr["qi"][1][:bd].reshape(bd, IDX_HEADS, IDX_DIM),
                             kiwi_f[:bd, IDX_DIM:IDX_DIM + IDX_HEADS, None], ki_new[:, None, :],
                             cache_kidx_t, l)
        k_new = pr["kf"][1][:bd].reshape(bd, N_HEADS, HEAD_DIM)
        v_new = pr["vf"][1][:bd].reshape(bd, N_HEADS, HEAD_DIM)
        att = _attn_sample(page_table, idx[:, 0],
                           pr["q"][1][:bd].astype(F32).reshape(bd, N_HEADS, HEAD_DIM),
                           k_new, v_new, cache_k, cache_v, l)
        ys_att = jnp.pad(att.reshape(bd, D_ATT), ((0, pad), (0, 0))).astype(BF16)
        outs["ks"].append(k_new[:, None])
        outs["vs"].append(v_new[:, None])
        outs["kis"].append(ki_new[:, None])
        outs["hs"].append(h_new[:bd])
        outs["cs"].append(jnp.concatenate([state_lru_conv[l][:, 1:], xg[:bd, None, :D_LRU]], axis=1))

        (xp, xp16), (xs, xs16) = _finish((xp, xs), (yp_lru, ys_lru), (yp_att, ys_att), w, ln, l, alpha)

    stack = lambda n: jnp.stack(outs[n])
    return (xp.reshape(bp, tp, D_MODEL), xs[:bd].reshape(bd, 1, D_MODEL),
            stack("kp"), stack("vp"), stack("kip"), stack("hp"), stack("cp"),
            stack("ks"), stack("vs"), stack("kis"), stack("hs"), stack("cs"))
```

```python
import functools
import math

import jax
import jax.numpy as jnp
from jax import lax
from jax.experimental import pallas as pl
from jax.experimental.pallas import tpu as pltpu

F32 = jnp.float32
BF16 = jnp.bfloat16
I32 = jnp.int32

D_MODEL = 2048
D_LRU = 1024
LRU_BLOCKS = 8
LRU_BW = 128
CONV_W = 4
LRU_C = 8.0
N_HEADS = 8
HEAD_DIM = 128
D_ATT = N_HEADS * HEAD_DIM
IDX_HEADS = 16
IDX_DIM = 64
D_QIDX = IDX_HEADS * IDX_DIM
D_FF = 5632
TOPK = 256
ROPE_THETA = 10000.0
PAGE = 128
LN_EPS = 1e-5
KIWI_W = 128
WI_SCALE = float(D_QIDX) ** -0.5
Q_SCALE = float(HEAD_DIM) ** -0.5 * math.log2(math.e)

O_LRU = 0
O_QKV = 2 * D_LRU
O_QI = O_QKV + 3 * D_ATT
O_KIWI = O_QI + D_QIDX

INT_MIN = -(2 ** 31)
NEG_BIG = -1e30
SAMPLE_ROWS = 16
ATT_SUB = 128
ROW_BLOCK = 256
VMEM_LIMIT = 56 * 1024 * 1024


def _cparams(semantics):
    return pltpu.CompilerParams(dimension_semantics=semantics, vmem_limit_bytes=VMEM_LIMIT)


def _dot(a, b):
    return jnp.dot(a, b, preferred_element_type=F32)


def _dot_nt(a, b):
    return lax.dot_general(a, b, (((1,), (1,)), ((), ())), preferred_element_type=F32)


def _sigmoid(x):
    return 1.0 / (1.0 + jnp.exp(-x))


def _rope(x, cos, sin, half):
    n = x.shape[-1]
    lane = lax.broadcasted_iota(I32, x.shape, 1)
    first = (lane & (2 * half - 1)) < half
    partner = jnp.where(first, pltpu.roll(x, n - half, 1), pltpu.roll(x, half, 1))
    reps = n // cos.shape[-1]
    if reps > 1:
        cos = jnp.tile(cos, (1, reps))
        sin = jnp.tile(sin, (1, reps))
    return x * cos + partner * sin


def _rope_tables(pos, head_dim):
    half = head_dim // 2
    inv = ROPE_THETA ** (-jnp.arange(half, dtype=F32) / half)
    ang = pos.astype(F32)[:, None] * inv[None, :]
    cos = jnp.cos(ang)
    sin = jnp.sin(ang)
    cos = jnp.concatenate([cos, cos], -1)
    sin = jnp.concatenate([-sin, sin], -1)
    reps = 128 // head_dim
    return jnp.tile(cos, (1, reps)), jnp.tile(sin, (1, reps))


def _stage_weight(w_ref, wb_ref):
    rb = min(w_ref.shape[0], ROW_BLOCK)
    for r in range(0, w_ref.shape[0], rb):
        wb_ref[r:r + rb, :] = w_ref[r:r + rb, :].astype(BF16)


def _row_blocks(tm):
    return [slice(r, r + min(tm, ROW_BLOCK)) for r in range(0, tm, ROW_BLOCK)]


def _dual_specs(tm, k, ms, tn, layer, col_block, nt):
    tab_p = pl.BlockSpec((tm, 128), lambda j, i: (i % nt, 0))
    tab_s = pl.BlockSpec((ms, 128), lambda j, i: (0, 0))
    return [pl.BlockSpec((tm, k), lambda j, i: (i, 0)),
            pl.BlockSpec((ms, k), lambda j, i: (0, 0)),
            pl.BlockSpec((None, tn, k), lambda j, i: (layer, col_block + j, 0)),
            tab_p, tab_p, tab_s, tab_s]


def _proj_plain_kernel(xp_ref, xs_ref, w_ref, op_ref, os_ref, wb_ref):
    @pl.when(pl.program_id(1) == 0)
    def _():
        _stage_weight(w_ref, wb_ref)
        os_ref[...] = _dot_nt(xs_ref[...], wb_ref[...])

    for rows in _row_blocks(xp_ref.shape[0]):
        op_ref[rows, :] = _dot_nt(xp_ref[rows, :], wb_ref[...])


def _proj_lru(xp, xs, w_in_t, layer, tm, tn):
    mp, k = xp.shape
    ms = xs.shape[0]
    n = 2 * D_LRU
    return pl.pallas_call(
        _proj_plain_kernel,
        grid=(n // tn, mp // tm),
        in_specs=_dual_specs(tm, k, ms, tn, layer, O_LRU // tn, 1)[:3],
        out_specs=[pl.BlockSpec((tm, tn), lambda j, i: (i, j)),
                   pl.BlockSpec((ms, tn), lambda j, i: (0, j))],
        out_shape=[jax.ShapeDtypeStruct((mp, n), F32), jax.ShapeDtypeStruct((ms, n), F32)],
        scratch_shapes=[pltpu.VMEM((tn, k), BF16)],
        compiler_params=_cparams(("parallel", "arbitrary")),
        name="proj_lru",
    )(xp, xs, w_in_t)


def _emit_heads(p, cos, sin, half, scale, of_ref, ob_ref, rows):
    if half:
        p = _rope(p, cos, sin, half)
    if of_ref is not None:
        of_ref[rows, :] = p
    if scale != 1.0:
        p = p * scale
    ob_ref[rows, :] = p.astype(BF16)


def _proj_heads_kernel(xp_ref, xs_ref, w_ref, cp_ref, sp_ref, cs_ref, ss_ref, *rest,
                       half, scale, keep_f32):
    if keep_f32:
        _, ofp_ref, obp_ref, ofs_ref, obs_ref, wb_ref = rest
    else:
        (obp_ref, obs_ref, wb_ref), ofp_ref, ofs_ref = rest, None, None

    @pl.when(pl.program_id(1) == 0)
    def _():
        _stage_weight(w_ref, wb_ref)
        _emit_heads(_dot_nt(xs_ref[...], wb_ref[...]), cs_ref[...], ss_ref[...], half, scale,
                    ofs_ref, obs_ref, slice(None))

    for rows in _row_blocks(xp_ref.shape[0]):
        _emit_heads(_dot_nt(xp_ref[rows, :], wb_ref[...]), cp_ref[rows, :], sp_ref[rows, :],
                    half, scale, ofp_ref, obp_ref, rows)


def _proj_heads(xp, xs, w_in_t, tabs_p, tabs_s, layer, col, tm, half, scale, keep_f32, name, stack=None):
    mp, k = xp.shape
    ms = xs.shape[0]
    tn = D_ATT
    nt = tabs_p[0].shape[0] // tm
    depth = w_in_t.shape[0]
    blk_p = pl.BlockSpec((tm, tn), lambda j, i: (i, 0))
    blk_s = pl.BlockSpec((ms, tn), lambda j, i: (0, 0))
    in_specs = _dual_specs(tm, k, ms, tn, layer, col // tn, nt)
    args = [xp, xs, w_in_t, *tabs_p, *tabs_s]
    if keep_f32:
        out_specs = [pl.BlockSpec((None, tm, tn), lambda j, i: (layer, i, 0)), blk_p, blk_s, blk_s]
        out_shape = [jax.ShapeDtypeStruct((depth, mp, tn), F32), jax.ShapeDtypeStruct((mp, tn), BF16),
                     jax.ShapeDtypeStruct((ms, tn), F32), jax.ShapeDtypeStruct((ms, tn), BF16)]
    else:
        out_specs = [blk_p, blk_s]
        out_shape = [jax.ShapeDtypeStruct((mp, tn), BF16), jax.ShapeDtypeStruct((ms, tn), BF16)]
    aliases = {}
    if keep_f32:
        in_specs = in_specs + [pl.BlockSpec(memory_space=pl.ANY)]
        args.append(stack)
        aliases = {len(args) - 1: 0}
    return pl.pallas_call(
        functools.partial(_proj_heads_kernel, half=half, scale=scale, keep_f32=keep_f32),
        grid=(1, mp // tm),
        in_specs=in_specs,
        out_specs=out_specs,
        out_shape=out_shape,
        scratch_shapes=[pltpu.VMEM((tn, k), BF16)],
        input_output_aliases=aliases,
        compiler_params=_cparams(("parallel", "arbitrary")),
        name=name,
    )(*args)


def _emit_kiwi(p, cos, sin, of_ref, ob_ref, rows):
    r = _rope(p, cos, sin, IDX_DIM // 2)
    lane = lax.broadcasted_iota(I32, p.shape, 1)
    of_ref[rows, :] = jnp.where(lane < IDX_DIM, r,
                                jnp.where(lane < IDX_DIM + IDX_HEADS, p * WI_SCALE, 0.0))
    ob_ref[rows, :] = jnp.where(lane < IDX_DIM, r, pltpu.roll(r, IDX_DIM, 1)).astype(BF16)


def _proj_indexer_kernel(xp_ref, xs_ref, w_ref, cp_ref, sp_ref, cs_ref, ss_ref, wk_ref,
                         qip_ref, kfp_ref, kbp_ref, qis_ref, kfs_ref, kbs_ref, wb_ref, wkb_ref):
    half = IDX_DIM // 2

    def emit(x, cos, sin, qi_ref, kf_ref, kb_ref, rows):
        _emit_heads(_dot_nt(x, wb_ref[...]), cos, sin, half, 1.0, None, qi_ref, rows)
        _emit_kiwi(_dot_nt(x, wkb_ref[...]), cos, sin, kf_ref, kb_ref, rows)

    @pl.when(pl.program_id(1) == 0)
    def _():
        _stage_weight(w_ref, wb_ref)
        _stage_weight(wk_ref, wkb_ref)
        emit(xs_ref[...], cs_ref[...], ss_ref[...], qis_ref, kfs_ref, kbs_ref, slice(None))

    for rows in _row_blocks(xp_ref.shape[0]):
        emit(xp_ref[rows, :], cp_ref[rows, :], sp_ref[rows, :], qip_ref, kfp_ref, kbp_ref, rows)


def _proj_indexer(xp, xs, w_in_t, w_kiwi_t, tabs_p, tabs_s, layer, tm):
    mp, k = xp.shape
    ms = xs.shape[0]
    nt = tabs_p[0].shape[0] // tm
    specs = lambda m, idx: [pl.BlockSpec((m, D_QIDX), idx), pl.BlockSpec((m, KIWI_W), idx),
                            pl.BlockSpec((m, KIWI_W), idx)]
    shapes = lambda m: [jax.ShapeDtypeStruct((m, D_QIDX), BF16), jax.ShapeDtypeStruct((m, KIWI_W), F32),
                        jax.ShapeDtypeStruct((m, KIWI_W), BF16)]
    return pl.pallas_call(
        _proj_indexer_kernel,
        grid=(1, mp // tm),
        in_specs=(_dual_specs(tm, k, ms, D_QIDX, layer, O_QI // D_QIDX, nt)
                  + [pl.BlockSpec((None, KIWI_W, k), lambda j, i: (layer, 0, 0))]),
        out_specs=specs(tm, lambda j, i: (i, 0)) + specs(ms, lambda j, i: (0, 0)),
        out_shape=shapes(mp) + shapes(ms),
        scratch_shapes=[pltpu.VMEM((D_QIDX, k), BF16), pltpu.VMEM((KIWI_W, k), BF16)],
        compiler_params=_cparams(("parallel", "arbitrary")),
        name="proj_indexer",
    )(xp, xs, w_in_t, *tabs_p, *tabs_s, w_kiwi_t)


def _gelu_tanh(x):
    return 0.5 * x * (1.0 + jnp.tanh(0.7978845608028654 * (x + 0.044715 * (x * x * x))))


def _lru_coeffs(xc, ga, gx, gab, gxb, lam):
    r = _sigmoid(ga + gab)
    i = _sigmoid(gx + gxb)
    z = -lam
    softplus = jnp.maximum(z, 0.0) + jnp.log(1.0 + jnp.exp(-jnp.abs(z)))
    log_a = (-LRU_C * softplus) * r
    a = jnp.exp(log_a)
    b = jnp.sqrt(1.0 - a * a) * (i * xc)
    return a, b


def _lru_prompt_kernel(xb_ref, gt_ref, cw_ref, cb_ref, gw_ref, gab_ref, gxb_ref, lam_ref,
                       y_ref, hl_ref, cl_ref, xext_ref, h_ref, a_ref, b_ref, *, tc):
    t = pl.program_id(2)

    @pl.when(t == 0)
    def _():
        xext_ref[0:8, :] = jnp.zeros((8, LRU_BW), F32)
        h_ref[...] = jnp.zeros_like(h_ref)

    x = xb_ref[...]
    xext_ref[8:8 + tc, :] = x
    cw = cw_ref[...]
    xc = (cb_ref[...] + cw[3:4] * x + cw[2:3] * xext_ref[7:7 + tc, :]
          + cw[1:2] * xext_ref[6:6 + tc, :] + cw[0:1] * xext_ref[5:5 + tc, :])
    cl_ref[0] = xext_ref[tc + 5:tc + 8, :]
    xext_ref[0:8, :] = x[tc - 8:tc, :]

    g = _dot(xc.astype(BF16), gw_ref[0])
    a, b = _lru_coeffs(xc, g[:, :LRU_BW], g[:, LRU_BW:], gab_ref[...], gxb_ref[...], lam_ref[...])

    def doubling(a, b, pos, n):
        d = 1
        while d < n:
            live = pos >= d
            b = jnp.where(live, a * pltpu.roll(b, d, 0) + b, b)
            a = jnp.where(live, a * pltpu.roll(a, d, 0), a)
            d *= 2
        return a, b

    ng = tc // 8
    row = lax.broadcasted_iota(I32, (tc, LRU_BW), 0)
    a, b = doubling(a, b, row & 7, 8)
    a_ref[...] = a
    b_ref[...] = b
    grow = lax.broadcasted_iota(I32, (ng, LRU_BW), 0)
    ga, gb = doubling(a_ref[pl.ds(7, ng, stride=8), :], b_ref[pl.ds(7, ng, stride=8), :], grow, ng)
    h_group = ga * h_ref[...] + gb
    h_in = jnp.where(grow == 0, h_ref[...], pltpu.roll(h_group, 1, 0))
    h = jnp.concatenate([a[8 * i:8 * i + 8, :] * h_in[i:i + 1, :] + b[8 * i:8 * i + 8, :]
                         for i in range(ng)], axis=0)
    h_ref[...] = h_group[ng - 1:ng, :]
    hl_ref[0] = h_group[ng - 1:ng, :]
    y_ref[...] = (h * _gelu_tanh(gt_ref[...])).astype(BF16)


def _lru_prompt(xg, conv_w, conv_b, gw, gab, gxb, lam, layer, batch, seq, tc):
    nt = seq // tc
    vec = pl.BlockSpec((1, LRU_BW), lambda b, n, t: (0, n))
    y, h_last, conv_last = pl.pallas_call(
        functools.partial(_lru_prompt_kernel, tc=tc),
        grid=(batch, LRU_BLOCKS, nt),
        in_specs=[pl.BlockSpec((tc, LRU_BW), lambda b, n, t: (b * nt + t, n)),
                  pl.BlockSpec((tc, LRU_BW), lambda b, n, t: (b * nt + t, LRU_BLOCKS + n)),
                  pl.BlockSpec((CONV_W, LRU_BW), lambda b, n, t: (0, n)),
                  vec,
                  pl.BlockSpec((None, 1, LRU_BW, 2 * LRU_BW), lambda b, n, t: (layer, n, 0, 0)),
                  vec, vec, vec],
        out_specs=[pl.BlockSpec((tc, LRU_BW), lambda b, n, t: (b * nt + t, n)),
                   pl.BlockSpec((1, 1, LRU_BW), lambda b, n, t: (b, 0, n)),
                   pl.BlockSpec((1, CONV_W - 1, LRU_BW), lambda b, n, t: (b, 0, n))],
        out_shape=[jax.ShapeDtypeStruct((batch * seq, D_LRU), BF16),
                   jax.ShapeDtypeStruct((batch, 1, D_LRU), F32),
                   jax.ShapeDtypeStruct((batch, CONV_W - 1, D_LRU), F32)],
        scratch_shapes=[pltpu.VMEM((tc + 8, LRU_BW), F32), pltpu.VMEM((1, LRU_BW), F32),
                        pltpu.VMEM((tc, LRU_BW), F32), pltpu.VMEM((tc, LRU_BW), F32)],
        compiler_params=_cparams(("parallel", "parallel", "arbitrary")),
        name="rglru_prompt",
    )(xg, xg, conv_w, conv_b, gw, gab, gxb, lam)
    return y, h_last[:, 0], conv_last


def _lru_sample_kernel(xg_ref, c0_ref, c1_ref, c2_ref, h0_ref, cw_ref, cb_ref, gw_ref,
                       gab_ref, gxb_ref, lam_ref, y_ref, h_ref):
    xb = xg_ref[:, :D_LRU]
    gate = xg_ref[:, D_LRU:]
    cw = cw_ref[...]
    xc = (cb_ref[...] + cw[0:1] * c0_ref[...] + cw[1:2] * c1_ref[...]
          + cw[2:3] * c2_ref[...] + cw[3:4] * xb)
    xcb = xc.astype(BF16)
    ga, gx = [], []
    for n in range(LRU_BLOCKS):
        g = _dot(xcb[:, n * LRU_BW:(n + 1) * LRU_BW], gw_ref[n])
        ga.append(g[:, :LRU_BW])
        gx.append(g[:, LRU_BW:])
    a, b = _lru_coeffs(xc, jnp.concatenate(ga, -1), jnp.concatenate(gx, -1),
                       gab_ref[...], gxb_ref[...], lam_ref[...])
    h = a * h0_ref[...] + b
    h_ref[...] = h
    y_ref[...] = (h * _gelu_tanh(gate)).astype(BF16)


def _lru_sample(xg, conv_state, h0, conv_w, conv_b, gw_l, gab, gxb, lam):
    rows = xg.shape[0]
    return pl.pallas_call(
        _lru_sample_kernel,
        out_shape=[jax.ShapeDtypeStruct((rows, D_LRU), BF16), jax.ShapeDtypeStruct((rows, D_LRU), F32)],
        compiler_params=pltpu.CompilerParams(vmem_limit_bytes=VMEM_LIMIT),
        name="rglru_sample",
    )(xg, conv_state[:, 0], conv_state[:, 1], conv_state[:, 2], h0, conv_w, conv_b, gw_l, gab, gxb, lam)


def _sortable_key(s):
    s = jnp.where(s == 0.0, 0.0, s)
    bits = pltpu.bitcast(s, I32)
    return bits ^ ((bits >> 31) & 0x7FFFFFFF)


def _bit_transpose32(words):
    a = list(words)
    j = 16
    m = 0x0000FFFF
    while j:
        k = 0
        while k < 32:
            t = (a[k] ^ lax.shift_right_logical(a[k + j], j)) & m
            a[k] = a[k] ^ t
            a[k + j] = a[k + j] ^ jnp.left_shift(t, j)
            k = (k + j + 1) & ~j
        j >>= 1
        m = (m ^ (m << j)) & 0xFFFFFFFF
    return a


def _attn_prompt_kernel(q_ref, qi_ref, wt_ref, k_ref, vt_ref, kd_ref, o_ref,
                        keys_ref, planes_ref, cand_ref, bias_ref, qm_ref, acc_ref, ml_ref, *, tq):
    i = pl.program_id(1)
    nk = i + 1
    ck = tq

    lane = lax.broadcasted_iota(I32, (tq, 128), 1)
    for p in range(IDX_HEADS // 2):
        slab = qi_ref[:, p * 128:(p + 1) * 128]
        zero = jnp.zeros_like(slab)
        qm_ref[2 * p] = jnp.where(lane < IDX_DIM, slab, zero)
        qm_ref[2 * p + 1] = jnp.where(lane >= IDX_DIM, slab, zero)

    qpos = i * tq + lax.broadcasted_iota(I32, (1, tq), 1)
    kidx = lax.broadcasted_iota(I32, (ck, tq), 0)

    def score_body(c, carry):
        off = pl.multiple_of(c * ck, ck)
        kd = kd_ref[0, pl.ds(off, ck), :]
        acc = jnp.zeros((ck, tq), F32)
        for h in range(IDX_HEADS):
            x = _dot_nt(kd, qm_ref[h])
            acc = acc + wt_ref[0, h:h + 1, :] * jnp.maximum(x, 0.0)
        key = jnp.where(kidx + off <= qpos, _sortable_key(acc), INT_MIN)
        keys_ref[pl.ds(off, ck), :] = key
        ukey = key ^ INT_MIN
        planes = _bit_transpose32([ukey[r * 8:(r + 1) * 8, :] for r in range(32)])
        for b in range(32):
            planes_ref[c, b] = planes[b]
        cand_ref[c] = jnp.full((8, tq), -1, I32)
        return carry

    lax.fori_loop(0, nk, score_body, 0)

    kk = jnp.minimum(qpos + 1, TOPK)

    def sweep(b, drop, first):
        def body(c, acc):
            cand = cand_ref[c]
            if not first:
                cand = cand & (planes_ref[c, b - 1] ^ drop)
                cand_ref[c] = cand
            return acc + lax.population_count(cand & planes_ref[c, b])
        acc = lax.fori_loop(0, nk, body, jnp.zeros((8, tq), I32))
        return acc.sum(axis=0, keepdims=True)

    def decide(b, n_ones, rem, t_u):
        take = n_ones >= rem
        t_u = jnp.where(take, t_u | jnp.left_shift(1, 31 - b), t_u)
        rem = jnp.where(take, rem, rem - n_ones)
        return rem, t_u, jnp.where(take, 0, -1)

    rem, t_u, drop = decide(0, sweep(0, None, True), kk, jnp.zeros((1, tq), I32))

    def bit_body(b, carry):
        rem, t_u, drop = carry
        return decide(b, sweep(b, drop, False), rem, t_u)

    rem, t_u, drop = lax.fori_loop(1, 32, bit_body, (rem, t_u, drop))

    def last_body(c, acc):
        return acc + lax.population_count(cand_ref[c] & (planes_ref[c, 31] ^ drop))

    n_eq = lax.fori_loop(0, nk, last_body, jnp.zeros((8, tq), I32)).sum(axis=0, keepdims=True)
    t = t_u ^ INT_MIN
    need = rem

    def count(pred):
        def body(c, acc):
            off = pl.multiple_of(c * ck, ck)
            m = jnp.where(pred(keys_ref[pl.ds(off, ck), :], off), 1, 0)
            return acc + m.reshape(ck // 8, 8, tq).sum(axis=0)
        acc = lax.fori_loop(0, nk, body, jnp.zeros((8, tq), I32))
        return acc.sum(axis=0, keepdims=True)

    tie = jnp.max(n_eq - need) > 0
    jbits = 13

    def j_body(it, jcut):
        j_try = jcut + jnp.left_shift(1, jbits - 1 - it)
        n = count(lambda kc, off: (kc == t) & (kidx + off < j_try))
        return jnp.where(n <= need, j_try, jcut)

    jcut = lax.fori_loop(0, jnp.where(tie, jbits, 0), j_body,
                         jnp.where(tie, 0, 2 ** jbits) + jnp.zeros((1, tq), I32))

    def bias_body(c, carry):
        off = pl.multiple_of(c * ck, ck)
        kc = keys_ref[pl.ds(off, ck), :]
        sel = (kc > t) | ((kc == t) & (kidx + off < jcut))
        bias_ref[pl.ds(off, ck), :] = jnp.where(sel, 0.0, NEG_BIG)
        return carry

    lax.fori_loop(0, nk, bias_body, 0)

    ml_ref[0:N_HEADS, :] = jnp.full((N_HEADS, tq), NEG_BIG, F32)
    ml_ref[N_HEADS:, :] = jnp.zeros((N_HEADS, tq), F32)
    acc_ref[...] = jnp.zeros_like(acc_ref)

    def att_body(c, carry):
        for j in range(ck // ATT_SUB):
            off = pl.multiple_of(c * ck + j * ATT_SUB, ATT_SUB)
            bias = bias_ref[pl.ds(off, ATT_SUB), :]
            for h in range(N_HEADS):
                hs = slice(h * HEAD_DIM, (h + 1) * HEAD_DIM)
                s = _dot_nt(k_ref[0, pl.ds(off, ATT_SUB), hs], q_ref[:, hs]) + bias
                m_old = ml_ref[h:h + 1, :]
                m_new = jnp.maximum(m_old, s.max(axis=0, keepdims=True))
                alpha = jnp.exp2(m_old - m_new)
                p = jnp.exp2(s - m_new)
                ml_ref[h:h + 1, :] = m_new
                ml_ref[N_HEADS + h:N_HEADS + h + 1, :] = (
                    ml_ref[N_HEADS + h:N_HEADS + h + 1, :] * alpha + p.sum(axis=0, keepdims=True))
                pv = _dot(vt_ref[0, hs, pl.ds(off, ATT_SUB)], p.astype(BF16))
                acc_ref[h] = acc_ref[h] * alpha + pv
        return carry

    lax.fori_loop(0, nk, att_body, 0)

    for h in range(N_HEADS):
        out = acc_ref[h] / ml_ref[N_HEADS + h:N_HEADS + h + 1, :]
        o_ref[:, h * HEAD_DIM:(h + 1) * HEAD_DIM] = out.T.astype(BF16)


def _attn_prompt(q, qi, wt, k, vt, kd, batch, seq, tq):
    nq = seq // tq
    rows = lambda b, i: (b * nq + i, 0)
    whole = lambda b, i: (b, 0, 0)
    return pl.pallas_call(
        functools.partial(_attn_prompt_kernel, tq=tq),
        grid=(batch, nq),
        in_specs=[pl.BlockSpec((tq, D_ATT), rows),
                  pl.BlockSpec((tq, D_QIDX), rows),
                  pl.BlockSpec((1, IDX_HEADS, tq), lambda b, i: (b, 0, i)),
                  pl.BlockSpec((1, seq, D_ATT), whole),
                  pl.BlockSpec((1, D_ATT, seq), whole),
                  pl.BlockSpec((1, seq, KIWI_W), whole)],
        out_specs=pl.BlockSpec((tq, D_ATT), rows),
        out_shape=jax.ShapeDtypeStruct((batch * seq, D_ATT), BF16),
        scratch_shapes=[pltpu.VMEM((seq, tq), I32),
                        pltpu.VMEM((nq, 32, 8, tq), I32),
                        pltpu.VMEM((nq, 8, tq), I32),
                        pltpu.VMEM((seq, tq), F32),
                        pltpu.VMEM((IDX_HEADS, tq, 128), BF16),
                        pltpu.VMEM((N_HEADS, HEAD_DIM, tq), F32),
                        pltpu.VMEM((2 * N_HEADS, tq), F32)],
        compiler_params=_cparams(("parallel", "arbitrary")),
        name="dsa_prompt",
    )(q, qi, wt, k, vt, kd)


def _select_sample_kernel(pt_ref, qi_ref, w_ref, kn_ref, cache_ref, idx_ref,
                          buf_ref, sem, sc_ref, *, layer, n_pages):
    b = pl.program_id(0)
    past = n_pages * PAGE

    def page_copy(p):
        return pltpu.make_async_copy(cache_ref.at[layer, pt_ref[b, p]], buf_ref.at[p], sem)

    def start(p, c):
        page_copy(p).start()
        return c

    lax.fori_loop(0, n_pages, start, 0)

    def wait(p, c):
        page_copy(p).wait()
        return c

    lax.fori_loop(0, n_pages, wait, 0)

    qi = qi_ref[0]
    w = w_ref[0]
    group = 8

    def score_body(g, c):
        for j in range(group):
            p = g * group + j
            x = _dot(qi, buf_ref[p].astype(BF16))
            sc_ref[pl.ds(p, 1), :] = (w * jnp.maximum(x, 0.0)).sum(axis=0, keepdims=True)
        return c

    lax.fori_loop(0, n_pages // group, score_body, 0)

    kn = kn_ref[0].astype(BF16).astype(F32)
    x_new = (qi.astype(F32) * kn).sum(axis=1, keepdims=True)
    s_new = (w * jnp.maximum(x_new, 0.0)).sum(axis=0, keepdims=True)

    keys = _sortable_key(sc_ref[...])
    key_new = _sortable_key(s_new)
    pos = (lax.broadcasted_iota(I32, (n_pages, PAGE), 0) * PAGE
           + lax.broadcasted_iota(I32, (n_pages, PAGE), 1))

    def count(m_past, m_new):
        return jnp.sum(jnp.where(m_past, 1, 0)) + jnp.sum(jnp.where(m_new, 1, 0))

    def enough(t_try):
        return count(keys >= t_try, key_new >= t_try) >= TOPK

    t = jnp.where(enough(0), 0, INT_MIN)
    t = jnp.where(enough(t | (1 << 30)), t | (1 << 30), t)

    def pair_body(it, t):
        lo = 28 - 2 * it
        t1, t2, t3 = (t | jnp.left_shift(v, lo) for v in (1, 2, 3))
        return jnp.where(enough(t3), t3, jnp.where(enough(t2), t2, jnp.where(enough(t1), t1, t)))

    t = lax.fori_loop(0, 15, pair_body, t)
    need = TOPK - count(keys > t, key_new > t)
    tie = count(keys == t, key_new == t) > need
    jbits = 15

    def j_body(it, jcut):
        j_try = jcut + jnp.left_shift(1, jbits - 1 - it)
        n = count((keys == t) & (pos < j_try), (key_new == t) & (past < j_try))
        return jnp.where(n <= need, j_try, jcut)

    jcut = lax.fori_loop(0, jnp.where(tie, jbits, 0), j_body, jnp.where(tie, 0, 2 ** jbits))
    sel = (keys > t) | ((keys == t) & (pos < jcut))
    n_past = jnp.sum(jnp.where(sel, 1, 0))

    selb = jnp.where(sel, 1.0, 0.0).astype(BF16)
    r_i = lax.broadcasted_iota(I32, (PAGE, PAGE), 0)
    c_i = lax.broadcasted_iota(I32, (PAGE, PAGE), 1)
    incl = _dot(selb, jnp.where(r_i <= c_i, 1.0, 0.0).astype(BF16))
    cnt = jnp.broadcast_to(incl[:, PAGE - 1:PAGE], (n_pages, TOPK))
    pr_i = lax.broadcasted_iota(I32, (n_pages, n_pages), 0)
    pc_i = lax.broadcasted_iota(I32, (n_pages, n_pages), 1)
    pinc = _dot(jnp.where(pc_i <= pr_i, 1.0, 0.0).astype(BF16), cnt.astype(BF16))
    slot = lax.broadcasted_iota(I32, (n_pages, TOPK), 1).astype(F32)
    before = pinc <= slot
    page = jnp.sum(jnp.where(before, 1, 0), axis=0, keepdims=True)
    pexc = jnp.sum(jnp.where(before, cnt, 0.0), axis=0, keepdims=True)
    rank = slot[0:1, :] - pexc
    onehot = jnp.where(lax.broadcasted_iota(I32, (n_pages, TOPK), 0) == page, 1.0, 0.0).astype(BF16)
    incl_sel = _dot(incl.T.astype(BF16), onehot)
    off = jnp.sum(jnp.where(incl_sel <= rank, 1, 0), axis=0, keepdims=True)
    slot_i = lax.broadcasted_iota(I32, (1, TOPK), 1)
    idx_ref[0] = jnp.where(slot_i < n_past, page * PAGE + off, past)


def _select_sample(page_table, qi, w, ki_new, cache_kidx_t, layer):
    bd, n_pages = page_table.shape
    grid_spec = pltpu.PrefetchScalarGridSpec(
        num_scalar_prefetch=1,
        grid=(bd,),
        in_specs=[pl.BlockSpec((1, IDX_HEADS, IDX_DIM), lambda b, pt: (b, 0, 0)),
                  pl.BlockSpec((1, IDX_HEADS, 1), lambda b, pt: (b, 0, 0)),
                  pl.BlockSpec((1, 1, IDX_DIM), lambda b, pt: (b, 0, 0)),
                  pl.BlockSpec(memory_space=pl.ANY)],
        out_specs=pl.BlockSpec((1, 1, TOPK), lambda b, pt: (b, 0, 0)),
        scratch_shapes=[pltpu.VMEM((n_pages, IDX_DIM, PAGE), F32),
                        pltpu.SemaphoreType.DMA(()),
                        pltpu.VMEM((n_pages, PAGE), F32)],
    )
    return pl.pallas_call(
        functools.partial(_select_sample_kernel, layer=layer, n_pages=n_pages),
        grid_spec=grid_spec,
        out_shape=jax.ShapeDtypeStruct((bd, 1, TOPK), I32),
        compiler_params=_cparams(("arbitrary",)),
        name="dsa_sample_select",
    )(page_table, qi, w, ki_new, cache_kidx_t)


def _attn_sample_kernel(pt_ref, idx_ref, q_ref, kn_ref, vn_ref, ck_ref, cv_ref, o_ref,
                        kbuf_ref, vbuf_ref, sems, *, layer, n_pages):
    b = pl.program_id(0)
    past = n_pages * PAGE

    def row_copies(r):
        s = jnp.minimum(idx_ref[b, r], past - 1)
        page = pt_ref[b, s // PAGE]
        off = s % PAGE
        return (pltpu.make_async_copy(ck_ref.at[layer, page, off], kbuf_ref.at[r], sems.at[0]),
                pltpu.make_async_copy(cv_ref.at[layer, page, off], vbuf_ref.at[r], sems.at[1]))

    def start(r, c):
        ck, cv = row_copies(r)
        ck.start()
        cv.start()
        return c

    lax.fori_loop(0, TOPK, start, 0, unroll=8)

    def wait(r, c):
        ck, cv = row_copies(r)
        ck.wait()
        cv.wait()
        return c

    lax.fori_loop(0, TOPK, wait, 0)

    @pl.when(idx_ref[b, TOPK - 1] == past)
    def _():
        kbuf_ref[TOPK - 1] = kn_ref[0]
        vbuf_ref[TOPK - 1] = vn_ref[0]

    q = q_ref[...]
    logits = (kbuf_ref[...] * q).sum(axis=-1, keepdims=True)
    m = logits.max(axis=0, keepdims=True)
    p = jnp.exp2(logits - m)
    den = p.sum(axis=0, keepdims=True)
    o_ref[...] = (p * vbuf_ref[...]).sum(axis=0, keepdims=True) / den


def _attn_sample(page_table, idx, q, k_new, v_new, cache_k, cache_v, layer):
    bd, n_pages = page_table.shape
    row = pl.BlockSpec((1, N_HEADS, HEAD_DIM), lambda b, pt, ix: (b, 0, 0))
    grid_spec = pltpu.PrefetchScalarGridSpec(
        num_scalar_prefetch=2,
        grid=(bd,),
        in_specs=[row, row, row, pl.BlockSpec(memory_space=pl.ANY), pl.BlockSpec(memory_space=pl.ANY)],
        out_specs=row,
        scratch_shapes=[pltpu.VMEM((TOPK, N_HEADS, HEAD_DIM), F32),
                        pltpu.VMEM((TOPK, N_HEADS, HEAD_DIM), F32),
                        pltpu.SemaphoreType.DMA((2,))],
    )
    return pl.pallas_call(
        functools.partial(_attn_sample_kernel, layer=layer, n_pages=n_pages),
        grid_spec=grid_spec,
        out_shape=jax.ShapeDtypeStruct((bd, N_HEADS, HEAD_DIM), F32),
        compiler_params=_cparams(("arbitrary",)),
        name="dsa_sample_attend",
    )(page_table, idx, q, k_new, v_new, cache_k, cache_v)


def _layer_norm(z, g, b):
    mu = jnp.mean(z, axis=-1, keepdims=True)
    zc = z - mu
    var = jnp.mean(zc * zc, axis=-1, keepdims=True)
    return zc * lax.rsqrt(var + LN_EPS) * g + b


def _mix_ln_kernel(yl_ref, ya_ref, w_ref, x_ref, g_ref, b_ref, of_ref, ob_ref, *, alpha):
    for rows in _row_blocks(x_ref.shape[0]):
        mix = _dot(yl_ref[rows, :], w_ref[:D_LRU, :]) + _dot(ya_ref[rows, :], w_ref[D_LRU:, :])
        y = _layer_norm(alpha * x_ref[rows, :] + mix, g_ref[...], b_ref[...])
        of_ref[rows, :] = y
        ob_ref[rows, :] = y.astype(BF16)


def _mix_ln(y_lru, y_att, w_out, x, g, b, layer, tm, alpha):
    m = x.shape[0]
    row = lambda i: (i, 0)
    fixed = lambda i: (0, 0)
    return pl.pallas_call(
        functools.partial(_mix_ln_kernel, alpha=alpha),
        grid=(m // tm,),
        in_specs=[pl.BlockSpec((tm, D_LRU), row), pl.BlockSpec((tm, D_ATT), row),
                  pl.BlockSpec((None, D_LRU + D_ATT, D_MODEL), lambda i: (layer, 0, 0)),
                  pl.BlockSpec((tm, D_MODEL), row),
                  pl.BlockSpec((1, D_MODEL), fixed), pl.BlockSpec((1, D_MODEL), fixed)],
        out_specs=[pl.BlockSpec((tm, D_MODEL), row), pl.BlockSpec((tm, D_MODEL), row)],
        out_shape=[jax.ShapeDtypeStruct((m, D_MODEL), F32), jax.ShapeDtypeStruct((m, D_MODEL), BF16)],
        compiler_params=_cparams(("parallel",)),
        name="mix_ln1",
    )(y_lru, y_att, w_out, x, g, b)


def _ffn_up_kernel(xp_ref, xs_ref, wg_ref, wu_ref, op_ref, os_ref, wgb_ref, wub_ref):
    def act(x):
        g = _dot(x, wgb_ref[...])
        u = _dot(x, wub_ref[...])
        return (g * _sigmoid(g) * u).astype(BF16)

    @pl.when(pl.program_id(1) == 0)
    def _():
        _stage_weight(wg_ref, wgb_ref)
        _stage_weight(wu_ref, wub_ref)
        os_ref[...] = act(xs_ref[...])

    tm = xp_ref.shape[0]
    for r in range(0, tm, ROW_BLOCK):
        op_ref[r:r + ROW_BLOCK, :] = act(xp_ref[r:r + ROW_BLOCK, :])


def _ffn_up(xp, xs, wg, wu, layer, tm, tn):
    mp, k = xp.shape
    ms = xs.shape[0]
    wspec = pl.BlockSpec((None, k, tn), lambda j, i: (layer, 0, j))
    return pl.pallas_call(
        _ffn_up_kernel,
        grid=(D_FF // tn, mp // tm),
        in_specs=[pl.BlockSpec((tm, k), lambda j, i: (i, 0)),
                  pl.BlockSpec((ms, k), lambda j, i: (0, 0)),
                  wspec, wspec],
        out_specs=[pl.BlockSpec((tm, tn), lambda j, i: (i, j)),
                   pl.BlockSpec((ms, tn), lambda j, i: (0, j))],
        out_shape=[jax.ShapeDtypeStruct((mp, D_FF), BF16), jax.ShapeDtypeStruct((ms, D_FF), BF16)],
        scratch_shapes=[pltpu.VMEM((k, tn), BF16), pltpu.VMEM((k, tn), BF16)],
        compiler_params=_cparams(("parallel", "arbitrary")),
        name="ffn_up",
    )(xp, xs, wg, wu)


def _ffn_down_ln_kernel(h_ref, w_ref, x_ref, g_ref, b_ref, of_ref, ob_ref, *, alpha):
    k = pl.program_id(1)
    last = pl.num_programs(1) - 1
    tm, tk = h_ref.shape
    blocks = [slice(r, r + min(tm, ROW_BLOCK)) for r in range(0, tm, ROW_BLOCK)]

    def part(rows):
        return _dot(h_ref[rows, :], w_ref[pl.ds(pl.multiple_of(k * tk, tk), tk), :])

    @pl.when(k == 0)
    def _():
        for rows in blocks:
            of_ref[rows, :] = part(rows)

    @pl.when((k > 0) & (k < last))
    def _():
        for rows in blocks:
            of_ref[rows, :] += part(rows)

    @pl.when(k == last)
    def _():
        for rows in blocks:
            z = alpha * x_ref[rows, :] + (of_ref[rows, :] + part(rows))
            y = _layer_norm(z, g_ref[...], b_ref[...])
            of_ref[rows, :] = y
            ob_ref[rows, :] = y.astype(BF16)


def _ffn_down_ln(h, wd, x, g, b, layer, tm, tk, alpha):
    m = x.shape[0]
    row = lambda i, k: (i, 0)
    fixed = lambda i, k: (0, 0)
    assert D_FF // tk >= 2
    return pl.pallas_call(
        functools.partial(_ffn_down_ln_kernel, alpha=alpha),
        grid=(m // tm, D_FF // tk),
        in_specs=[pl.BlockSpec((tm, tk), lambda i, k: (i, k)),
                  pl.BlockSpec((None, D_FF, D_MODEL), lambda i, k: (layer, 0, 0),
                               pipeline_mode=pl.Buffered(1)),
                  pl.BlockSpec((tm, D_MODEL), row),
                  pl.BlockSpec((1, D_MODEL), fixed), pl.BlockSpec((1, D_MODEL), fixed)],
        out_specs=[pl.BlockSpec((tm, D_MODEL), row), pl.BlockSpec((tm, D_MODEL), row)],
        out_shape=[jax.ShapeDtypeStruct((m, D_MODEL), F32), jax.ShapeDtypeStruct((m, D_MODEL), BF16)],
        compiler_params=_cparams(("parallel", "arbitrary")),
        name="ffn_down_ln2",
    )(h, wd, x, g, b)


def _tiles(rows):
    if rows <= SAMPLE_ROWS:
        return dict(mix=rows, down=rows)
    return dict(proj=1024, mix=512, up=1024, down=512)


def _project(xp16, xs16, w, layer, tabs_p, tabs_s, k_stack, v_stack):
    tm = _tiles(xp16.shape[0])["proj"]
    heads = functools.partial(_proj_heads, xp16, xs16, w["in_t"], layer=layer, tm=tm)
    t128 = (tabs_p[:2], tabs_s[:2])
    t64 = (tabs_p[2:], tabs_s[2:])
    out = {}
    out["xg"] = _proj_lru(xp16, xs16, w["in_t"], layer, tm, 1024)
    out["q"] = heads(*t128, col=O_QKV, half=HEAD_DIM // 2, scale=Q_SCALE, keep_f32=False, name="proj_q")
    kfp, kbp, kfs, kbs = heads(*t128, col=O_QKV + D_ATT, half=HEAD_DIM // 2, scale=1.0,
                               keep_f32=True, name="proj_k", stack=k_stack)
    vfp, vbp, vfs, vbs = heads(*t128, col=O_QKV + 2 * D_ATT, half=0, scale=1.0,
                               keep_f32=True, name="proj_v", stack=v_stack)
    out["kf"], out["kb"], out["vf"], out["vb"] = (kfp, kfs), (kbp, kbs), (vfp, vfs), (vbp, vbs)
    qip, fp, bp_, qis, fs, bs = _proj_indexer(xp16, xs16, w["in_t"], w["kiwi_t"], *t64, layer, tm)
    out["qi"], out["kiwi_f"], out["kiwi_b"] = (qip, qis), (fp, fs), (bp_, bs)
    return out


def _finish(xs, ys_lru, ys_att, w, ln, layer, alpha):
    x1, x1b = [], []
    for x, y_lru, y_att in zip(xs, ys_lru, ys_att):
        a, b = _mix_ln(y_lru, y_att, w["out"], x, ln[0], ln[1], layer, _tiles(x.shape[0])["mix"], alpha)
        x1.append(a)
        x1b.append(b)
    hs = _ffn_up(x1b[0], x1b[1], w["ffn_gate"], w["ffn_up"], layer, _tiles(xs[0].shape[0])["up"], 512)
    return [_ffn_down_ln(h, w["ffn_down"], x, ln[2], ln[3], layer, _tiles(x.shape[0])["down"],
                         D_FF // 2, alpha) for h, x in zip(hs, x1)]


def kernel(x_prompt, x_sample, cache_k, cache_v, cache_kidx, state_lru_h, state_lru_conv, page_table,
           w_in, conv_w, conv_b, gate_a_w, gate_a_b, gate_x_w, gate_x_b, lru_lambda, w_out,
           ln1_g, ln1_b, w_ffn_gate, w_ffn_up, w_ffn_down, ln2_g, ln2_b):
    depth = w_in.shape[0]
    bp, tp, _ = x_prompt.shape
    bd = x_sample.shape[0]
    n_pages = page_table.shape[1]
    past = n_pages * PAGE
    alpha = (2.0 * depth) ** 0.25
    pad = SAMPLE_ROWS - bd

    pos_p = jnp.arange(tp, dtype=I32)
    pos_s = jnp.full((SAMPLE_ROWS,), past, I32)
    tabs_p = _rope_tables(pos_p, HEAD_DIM) + _rope_tables(pos_p, IDX_DIM)
    tabs_s = _rope_tables(pos_s, HEAD_DIM) + _rope_tables(pos_s, IDX_DIM)

    xp = x_prompt.reshape(bp * tp, D_MODEL)
    xs = jnp.pad(x_sample.reshape(bd, D_MODEL), ((0, pad), (0, 0)))
    xp16 = xp.astype(BF16)
    xs16 = xs.astype(BF16)

    w_in_t = jnp.swapaxes(w_in, 1, 2)
    n_kiwi = w_in.shape[2] - O_KIWI
    w = {
        "in_t": w_in_t,
        "kiwi_t": jnp.pad(w_in_t[:, O_KIWI:, :], ((0, 0), (0, KIWI_W - n_kiwi), (0, 0))),
        "out": w_out.astype(BF16),
        "ffn_gate": w_ffn_gate,
        "ffn_up": w_ffn_up,
        "ffn_down": w_ffn_down.astype(BF16),
    }
    gw = jnp.concatenate([gate_a_w, gate_x_w], axis=-1).astype(BF16)
    cache_kidx_t = jnp.swapaxes(cache_kidx, 2, 3)

    outs = {n: [] for n in ("kip", "hp", "cp", "ks", "vs", "kis", "hs", "cs")}
    k_stack = jnp.zeros((depth, bp * tp, D_ATT), F32)
    v_stack = jnp.zeros((depth, bp * tp, D_ATT), F32)
    for l in range(depth):
        ln = (ln1_g[l][None], ln1_b[l][None], ln2_g[l][None], ln2_b[l][None])
        lru_vecs = (conv_w[l], conv_b[l][None])
        gate_vecs = (gate_a_b[l][None], gate_x_b[l][None], lru_lambda[l][None])
        pr = _project(xp16, xs16, w, l, tabs_p, tabs_s, k_stack, v_stack)
        k_stack, v_stack = pr["kf"][0], pr["vf"][0]

        kiwi_f = pr["kiwi_f"][0]
        yp_lru, h_last, conv_last = _lru_prompt(pr["xg"][0], *lru_vecs, gw, *gate_vecs, layer=l,
                                                batch=bp, seq=tp, tc=512)
        wt = jnp.swapaxes(kiwi_f.reshape(bp, tp, KIWI_W)[:, :, IDX_DIM:IDX_DIM + IDX_HEADS], 1, 2)
        vt = jnp.swapaxes(pr["vb"][0].reshape(bp, tp, D_ATT), 1, 2)
        yp_att = _attn_prompt(pr["q"][0], pr["qi"][0], wt, pr["kb"][0].reshape(bp, tp, D_ATT), vt,
                              pr["kiwi_b"][0].reshape(bp, tp, KIWI_W), batch=bp, seq=tp, tq=256)
        outs["kip"].append(kiwi_f[:, :IDX_DIM].reshape(bp, tp, IDX_DIM))
        outs["hp"].append(h_last)
        outs["cp"].append(conv_last)

        xg, kiwi_f = pr["xg"][1], pr["kiwi_f"][1]
        conv_state = jnp.pad(state_lru_conv[l], ((0, pad), (0, 0), (0, 0)))
        h0 = jnp.pad(state_lru_h[l], ((0, pad), (0, 0)))
        ys_lru, h_new = _lru_sample(xg, conv_state, h0, *lru_vecs, gw[l], *gate_vecs)
        ki_new = kiwi_f[:bd, :IDX_DIM]
        idx = _select_sample(page_table, pr["qi"][1][:bd].reshape(bd, IDX_HEADS, IDX_DIM),
                             kiwi_f[:bd, IDX_DIM:IDX_DIM + IDX_HEADS, None], ki_new[:, None, :],
                             cache_kidx_t, l)
        k_new = pr["kf"][1][:bd].reshape(bd, N_HEADS, HEAD_DIM)
        v_new = pr["vf"][1][:bd].reshape(bd, N_HEADS, HEAD_DIM)
        att = _attn_sample(page_table, idx[:, 0],
                           pr["q"][1][:bd].astype(F32).reshape(bd, N_HEADS, HEAD_DIM),
                           k_new, v_new, cache_k, cache_v, l)
        ys_att = jnp.pad(att.reshape(bd, D_ATT), ((0, pad), (0, 0))).astype(BF16)
        outs["ks"].append(k_new[:, None])
        outs["vs"].append(v_new[:, None])
        outs["kis"].append(ki_new[:, None])
        outs["hs"].append(h_new[:bd])
        outs["cs"].append(jnp.concatenate([state_lru_conv[l][:, 1:], xg[:bd, None, :D_LRU]], axis=1))

        (xp, xp16), (xs, xs16) = _finish((xp, xs), (yp_lru, ys_lru), (yp_att, ys_att), w, ln, l, alpha)

    stack = lambda n: jnp.stack(outs[n])
    return (xp.reshape(bp, tp, D_MODEL), xs[:bd].reshape(bd, 1, D_MODEL),
            k_stack.reshape(depth, bp, tp, N_HEADS, HEAD_DIM),
            v_stack.reshape(depth, bp, tp, N_HEADS, HEAD_DIM), stack("kip"), stack("hp"), stack("cp"),
            stack("ks"), stack("vs"), stack("kis"), stack("hs"), stack("cs"))
```

```python
import functools
import math

import jax
import jax.numpy as jnp
from jax import lax
from jax.experimental import pallas as pl
from jax.experimental.pallas import tpu as pltpu

F32 = jnp.float32
BF16 = jnp.bfloat16
I32 = jnp.int32

D_MODEL = 2048
D_LRU = 1024
LRU_BLOCKS = 8
LRU_BW = 128
CONV_W = 4
LRU_C = 8.0
N_HEADS = 8
HEAD_DIM = 128
D_ATT = N_HEADS * HEAD_DIM
IDX_HEADS = 16
IDX_DIM = 64
D_QIDX = IDX_HEADS * IDX_DIM
D_FF = 5632
TOPK = 256
ROPE_THETA = 10000.0
PAGE = 128
LN_EPS = 1e-5
KIWI_W = 128
WI_SCALE = float(D_QIDX) ** -0.5
Q_SCALE = float(HEAD_DIM) ** -0.5 * math.log2(math.e)

O_LRU = 0
O_QKV = 2 * D_LRU
O_QI = O_QKV + 3 * D_ATT
O_KIWI = O_QI + D_QIDX

INT_MIN = -(2 ** 31)
NEG_BIG = -1e30
SAMPLE_ROWS = 16
ATT_SUB = 128
ROW_BLOCK = 256
VMEM_LIMIT = 56 * 1024 * 1024


def _cparams(semantics):
    return pltpu.CompilerParams(dimension_semantics=semantics, vmem_limit_bytes=VMEM_LIMIT)


def _dot(a, b):
    return jnp.dot(a, b, preferred_element_type=F32)


def _dot_nt(a, b):
    return lax.dot_general(a, b, (((1,), (1,)), ((), ())), preferred_element_type=F32)


def _sigmoid(x):
    return 1.0 / (1.0 + jnp.exp(-x))


def _rope(x, cos, sin, half):
    n = x.shape[-1]
    lane = lax.broadcasted_iota(I32, x.shape, 1)
    first = (lane & (2 * half - 1)) < half
    partner = jnp.where(first, pltpu.roll(x, n - half, 1), pltpu.roll(x, half, 1))
    reps = n // cos.shape[-1]
    if reps > 1:
        cos = jnp.tile(cos, (1, reps))
        sin = jnp.tile(sin, (1, reps))
    return x * cos + partner * sin


def _rope_tables(pos, head_dim):
    half = head_dim // 2
    inv = ROPE_THETA ** (-jnp.arange(half, dtype=F32) / half)
    ang = pos.astype(F32)[:, None] * inv[None, :]
    cos = jnp.cos(ang)
    sin = jnp.sin(ang)
    cos = jnp.concatenate([cos, cos], -1)
    sin = jnp.concatenate([-sin, sin], -1)
    reps = 128 // head_dim
    return jnp.tile(cos, (1, reps)), jnp.tile(sin, (1, reps))


def _stage_weight(w_ref, wb_ref):
    rb = min(w_ref.shape[0], ROW_BLOCK)
    for r in range(0, w_ref.shape[0], rb):
        wb_ref[r:r + rb, :] = w_ref[r:r + rb, :].astype(BF16)


def _row_blocks(tm):
    return [slice(r, r + min(tm, ROW_BLOCK)) for r in range(0, tm, ROW_BLOCK)]


def _dual_specs(tm, k, ms, tn, layer, col_block, nt):
    tab_p = pl.BlockSpec((tm, 128), lambda j, i: (i % nt, 0))
    tab_s = pl.BlockSpec((ms, 128), lambda j, i: (0, 0))
    return [pl.BlockSpec((tm, k), lambda j, i: (i, 0)),
            pl.BlockSpec((ms, k), lambda j, i: (0, 0)),
            pl.BlockSpec((None, tn, k), lambda j, i: (layer, col_block + j, 0)),
            tab_p, tab_p, tab_s, tab_s]


def _proj_plain_kernel(xp_ref, xs_ref, w_ref, op_ref, os_ref, wb_ref):
    @pl.when(pl.program_id(1) == 0)
    def _():
        _stage_weight(w_ref, wb_ref)
        os_ref[...] = _dot_nt(xs_ref[...], wb_ref[...])

    for rows in _row_blocks(xp_ref.shape[0]):
        op_ref[rows, :] = _dot_nt(xp_ref[rows, :], wb_ref[...])


def _proj_lru(xp, xs, w_in_t, layer, tm, tn):
    mp, k = xp.shape
    ms = xs.shape[0]
    n = 2 * D_LRU
    return pl.pallas_call(
        _proj_plain_kernel,
        grid=(n // tn, mp // tm),
        in_specs=_dual_specs(tm, k, ms, tn, layer, O_LRU // tn, 1)[:3],
        out_specs=[pl.BlockSpec((tm, tn), lambda j, i: (i, j)),
                   pl.BlockSpec((ms, tn), lambda j, i: (0, j))],
        out_shape=[jax.ShapeDtypeStruct((mp, n), F32), jax.ShapeDtypeStruct((ms, n), F32)],
        scratch_shapes=[pltpu.VMEM((tn, k), BF16)],
        compiler_params=_cparams(("parallel", "arbitrary")),
        name="proj_lru",
    )(xp, xs, w_in_t)


def _emit_heads(p, cos, sin, half, scale, of_ref, ob_ref, rows):
    if half:
        p = _rope(p, cos, sin, half)
    if of_ref is not None:
        of_ref[rows, :] = p
    if scale != 1.0:
        p = p * scale
    ob_ref[rows, :] = p.astype(BF16)


def _proj_heads_kernel(xp_ref, xs_ref, w_ref, cp_ref, sp_ref, cs_ref, ss_ref, *rest,
                       half, scale, keep_f32):
    if keep_f32:
        _, ofp_ref, obp_ref, ofs_ref, obs_ref, wb_ref = rest
    else:
        (obp_ref, obs_ref, wb_ref), ofp_ref, ofs_ref = rest, None, None

    @pl.when(pl.program_id(1) == 0)
    def _():
        _stage_weight(w_ref, wb_ref)
        _emit_heads(_dot_nt(xs_ref[...], wb_ref[...]), cs_ref[...], ss_ref[...], half, scale,
                    ofs_ref, obs_ref, slice(None))

    for rows in _row_blocks(xp_ref.shape[0]):
        _emit_heads(_dot_nt(xp_ref[rows, :], wb_ref[...]), cp_ref[rows, :], sp_ref[rows, :],
                    half, scale, ofp_ref, obp_ref, rows)


def _proj_heads(xp, xs, w_in_t, tabs_p, tabs_s, layer, col, tm, half, scale, keep_f32, name, stack=None):
    mp, k = xp.shape
    ms = xs.shape[0]
    tn = D_ATT
    nt = tabs_p[0].shape[0] // tm
    depth = w_in_t.shape[0]
    blk_p = pl.BlockSpec((tm, tn), lambda j, i: (i, 0))
    blk_s = pl.BlockSpec((ms, tn), lambda j, i: (0, 0))
    in_specs = _dual_specs(tm, k, ms, tn, layer, col // tn, nt)
    args = [xp, xs, w_in_t, *tabs_p, *tabs_s]
    if keep_f32:
        out_specs = [pl.BlockSpec((None, tm, tn), lambda j, i: (layer, i, 0)), blk_p, blk_s, blk_s]
        out_shape = [jax.ShapeDtypeStruct((depth, mp, tn), F32), jax.ShapeDtypeStruct((mp, tn), BF16),
                     jax.ShapeDtypeStruct((ms, tn), F32), jax.ShapeDtypeStruct((ms, tn), BF16)]
    else:
        out_specs = [blk_p, blk_s]
        out_shape = [jax.ShapeDtypeStruct((mp, tn), BF16), jax.ShapeDtypeStruct((ms, tn), BF16)]
    aliases = {}
    if keep_f32:
        in_specs = in_specs + [pl.BlockSpec(memory_space=pl.ANY)]
        args.append(stack)
        aliases = {len(args) - 1: 0}
    return pl.pallas_call(
        functools.partial(_proj_heads_kernel, half=half, scale=scale, keep_f32=keep_f32),
        grid=(1, mp // tm),
        in_specs=in_specs,
        out_specs=out_specs,
        out_shape=out_shape,
        scratch_shapes=[pltpu.VMEM((tn, k), BF16)],
        input_output_aliases=aliases,
        compiler_params=_cparams(("parallel", "arbitrary")),
        name=name,
    )(*args)


def _emit_kiwi(p, cos, sin, of_ref, ob_ref, rows):
    r = _rope(p, cos, sin, IDX_DIM // 2)
    lane = lax.broadcasted_iota(I32, p.shape, 1)
    of_ref[rows, :] = jnp.where(lane < IDX_DIM, r,
                                jnp.where(lane < IDX_DIM + IDX_HEADS, p * WI_SCALE, 0.0))
    ob_ref[rows, :] = jnp.where(lane < IDX_DIM, r, pltpu.roll(r, IDX_DIM, 1)).astype(BF16)


def _proj_indexer_kernel(xp_ref, xs_ref, w_ref, cp_ref, sp_ref, cs_ref, ss_ref, wk_ref,
                         qip_ref, kfp_ref, kbp_ref, qis_ref, kfs_ref, kbs_ref, wb_ref, wkb_ref):
    half = IDX_DIM // 2

    def emit(x, cos, sin, qi_ref, kf_ref, kb_ref, rows):
        _emit_heads(_dot_nt(x, wb_ref[...]), cos, sin, half, 1.0, None, qi_ref, rows)
        _emit_kiwi(_dot_nt(x, wkb_ref[...]), cos, sin, kf_ref, kb_ref, rows)

    @pl.when(pl.program_id(1) == 0)
    def _():
        _stage_weight(w_ref, wb_ref)
        _stage_weight(wk_ref, wkb_ref)
        emit(xs_ref[...], cs_ref[...], ss_ref[...], qis_ref, kfs_ref, kbs_ref, slice(None))

    for rows in _row_blocks(xp_ref.shape[0]):
        emit(xp_ref[rows, :], cp_ref[rows, :], sp_ref[rows, :], qip_ref, kfp_ref, kbp_ref, rows)


def _proj_indexer(xp, xs, w_in_t, w_kiwi_t, tabs_p, tabs_s, layer, tm):
    mp, k = xp.shape
    ms = xs.shape[0]
    nt = tabs_p[0].shape[0] // tm
    specs = lambda m, idx: [pl.BlockSpec((m, D_QIDX), idx), pl.BlockSpec((m, KIWI_W), idx),
                            pl.BlockSpec((m, KIWI_W), idx)]
    shapes = lambda m: [jax.ShapeDtypeStruct((m, D_QIDX), BF16), jax.ShapeDtypeStruct((m, KIWI_W), F32),
                        jax.ShapeDtypeStruct((m, KIWI_W), BF16)]
    return pl.pallas_call(
        _proj_indexer_kernel,
        grid=(1, mp // tm),
        in_specs=(_dual_specs(tm, k, ms, D_QIDX, layer, O_QI // D_QIDX, nt)
                  + [pl.BlockSpec((None, KIWI_W, k), lambda j, i: (layer, 0, 0))]),
        out_specs=specs(tm, lambda j, i: (i, 0)) + specs(ms, lambda j, i: (0, 0)),
        out_shape=shapes(mp) + shapes(ms),
        scratch_shapes=[pltpu.VMEM((D_QIDX, k), BF16), pltpu.VMEM((KIWI_W, k), BF16)],
        compiler_params=_cparams(("parallel", "arbitrary")),
        name="proj_indexer",
    )(xp, xs, w_in_t, *tabs_p, *tabs_s, w_kiwi_t)


def _gelu_tanh(x):
    return 0.5 * x * (1.0 + jnp.tanh(0.7978845608028654 * (x + 0.044715 * (x * x * x))))


def _lru_coeffs(xc, ga, gx, gab, gxb, lam):
    r = _sigmoid(ga + gab)
    i = _sigmoid(gx + gxb)
    z = -lam
    softplus = jnp.maximum(z, 0.0) + jnp.log(1.0 + jnp.exp(-jnp.abs(z)))
    log_a = (-LRU_C * softplus) * r
    a = jnp.exp(log_a)
    b = jnp.sqrt(1.0 - a * a) * (i * xc)
    return a, b


def _lru_prompt_kernel(xb_ref, gt_ref, cw_ref, cb_ref, gw_ref, gab_ref, gxb_ref, lam_ref,
                       y_ref, hl_ref, cl_ref, xext_ref, h_ref, a_ref, b_ref, *, tc):
    t = pl.program_id(2)

    @pl.when(t == 0)
    def _():
        xext_ref[0:8, :] = jnp.zeros((8, LRU_BW), F32)
        h_ref[...] = jnp.zeros_like(h_ref)

    x = xb_ref[...]
    xext_ref[8:8 + tc, :] = x
    cw = cw_ref[...]
    xc = (cb_ref[...] + cw[3:4] * x + cw[2:3] * xext_ref[7:7 + tc, :]
          + cw[1:2] * xext_ref[6:6 + tc, :] + cw[0:1] * xext_ref[5:5 + tc, :])
    cl_ref[0] = xext_ref[tc + 5:tc + 8, :]
    xext_ref[0:8, :] = x[tc - 8:tc, :]

    g = _dot(xc.astype(BF16), gw_ref[0])
    a, b = _lru_coeffs(xc, g[:, :LRU_BW], g[:, LRU_BW:], gab_ref[...], gxb_ref[...], lam_ref[...])

    def doubling(a, b, pos, n):
        d = 1
        while d < n:
            live = pos >= d
            b = jnp.where(live, a * pltpu.roll(b, d, 0) + b, b)
            a = jnp.where(live, a * pltpu.roll(a, d, 0), a)
            d *= 2
        return a, b

    ng = tc // 8
    row = lax.broadcasted_iota(I32, (tc, LRU_BW), 0)
    a, b = doubling(a, b, row & 7, 8)
    a_ref[...] = a
    b_ref[...] = b
    grow = lax.broadcasted_iota(I32, (ng, LRU_BW), 0)
    ga, gb = doubling(a_ref[pl.ds(7, ng, stride=8), :], b_ref[pl.ds(7, ng, stride=8), :], grow, ng)
    h_group = ga * h_ref[...] + gb
    h_in = jnp.where(grow == 0, h_ref[...], pltpu.roll(h_group, 1, 0))
    h = jnp.concatenate([a[8 * i:8 * i + 8, :] * h_in[i:i + 1, :] + b[8 * i:8 * i + 8, :]
                         for i in range(ng)], axis=0)
    h_ref[...] = h_group[ng - 1:ng, :]
    hl_ref[0] = h_group[ng - 1:ng, :]
    y_ref[...] = (h * _gelu_tanh(gt_ref[...])).astype(BF16)


def _lru_prompt(xg, conv_w, conv_b, gw, gab, gxb, lam, layer, batch, seq, tc):
    nt = seq // tc
    vec = pl.BlockSpec((1, LRU_BW), lambda b, n, t: (0, n))
    y, h_last, conv_last = pl.pallas_call(
        functools.partial(_lru_prompt_kernel, tc=tc),
        grid=(batch, LRU_BLOCKS, nt),
        in_specs=[pl.BlockSpec((tc, LRU_BW), lambda b, n, t: (b * nt + t, n)),
                  pl.BlockSpec((tc, LRU_BW), lambda b, n, t: (b * nt + t, LRU_BLOCKS + n)),
                  pl.BlockSpec((CONV_W, LRU_BW), lambda b, n, t: (0, n)),
                  vec,
                  pl.BlockSpec((None, 1, LRU_BW, 2 * LRU_BW), lambda b, n, t: (layer, n, 0, 0)),
                  vec, vec, vec],
        out_specs=[pl.BlockSpec((tc, LRU_BW), lambda b, n, t: (b * nt + t, n)),
                   pl.BlockSpec((1, 1, LRU_BW), lambda b, n, t: (b, 0, n)),
                   pl.BlockSpec((1, CONV_W - 1, LRU_BW), lambda b, n, t: (b, 0, n))],
        out_shape=[jax.ShapeDtypeStruct((batch * seq, D_LRU), BF16),
                   jax.ShapeDtypeStruct((batch, 1, D_LRU), F32),
                   jax.ShapeDtypeStruct((batch, CONV_W - 1, D_LRU), F32)],
        scratch_shapes=[pltpu.VMEM((tc + 8, LRU_BW), F32), pltpu.VMEM((1, LRU_BW), F32),
                        pltpu.VMEM((tc, LRU_BW), F32), pltpu.VMEM((tc, LRU_BW), F32)],
        compiler_params=_cparams(("parallel", "parallel", "arbitrary")),
        name="rglru_prompt",
    )(xg, xg, conv_w, conv_b, gw, gab, gxb, lam)
    return y, h_last[:, 0], conv_last


def _lru_sample_kernel(xg_ref, c0_ref, c1_ref, c2_ref, h0_ref, cw_ref, cb_ref, gw_ref,
                       gab_ref, gxb_ref, lam_ref, y_ref, h_ref):
    xb = xg_ref[:, :D_LRU]
    gate = xg_ref[:, D_LRU:]
    cw = cw_ref[...]
    xc = (cb_ref[...] + cw[0:1] * c0_ref[...] + cw[1:2] * c1_ref[...]
          + cw[2:3] * c2_ref[...] + cw[3:4] * xb)
    xcb = xc.astype(BF16)
    ga, gx = [], []
    for n in range(LRU_BLOCKS):
        g = _dot(xcb[:, n * LRU_BW:(n + 1) * LRU_BW], gw_ref[n])
        ga.append(g[:, :LRU_BW])
        gx.append(g[:, LRU_BW:])
    a, b = _lru_coeffs(xc, jnp.concatenate(ga, -1), jnp.concatenate(gx, -1),
                       gab_ref[...], gxb_ref[...], lam_ref[...])
    h = a * h0_ref[...] + b
    h_ref[...] = h
    y_ref[...] = (h * _gelu_tanh(gate)).astype(BF16)


def _lru_sample(xg, conv_state, h0, conv_w, conv_b, gw_l, gab, gxb, lam):
    rows = xg.shape[0]
    return pl.pallas_call(
        _lru_sample_kernel,
        out_shape=[jax.ShapeDtypeStruct((rows, D_LRU), BF16), jax.ShapeDtypeStruct((rows, D_LRU), F32)],
        compiler_params=pltpu.CompilerParams(vmem_limit_bytes=VMEM_LIMIT),
        name="rglru_sample",
    )(xg, conv_state[:, 0], conv_state[:, 1], conv_state[:, 2], h0, conv_w, conv_b, gw_l, gab, gxb, lam)


def _sortable_key(s):
    s = jnp.where(s == 0.0, 0.0, s)
    bits = pltpu.bitcast(s, I32)
    return bits ^ ((bits >> 31) & 0x7FFFFFFF)


def _bit_transpose32(words):
    a = list(words)
    j = 16
    m = 0x0000FFFF
    while j:
        k = 0
        while k < 32:
            t = (a[k] ^ lax.shift_right_logical(a[k + j], j)) & m
            a[k] = a[k] ^ t
            a[k + j] = a[k + j] ^ jnp.left_shift(t, j)
            k = (k + j + 1) & ~j
        j >>= 1
        m = (m ^ (m << j)) & 0xFFFFFFFF
    return a


def _attn_prompt_kernel(q_ref, qi_ref, wt_ref, k_ref, vt_ref, kd_ref, o_ref,
                        keys_ref, planes_ref, cand_ref, bias_ref, qm_ref, acc_ref, ml_ref, *, tq):
    i = pl.program_id(1)
    nk = i + 1
    ck = tq

    lane = lax.broadcasted_iota(I32, (tq, 128), 1)
    for p in range(IDX_HEADS // 2):
        slab = qi_ref[:, p * 128:(p + 1) * 128]
        zero = jnp.zeros_like(slab)
        qm_ref[2 * p] = jnp.where(lane < IDX_DIM, slab, zero)
        qm_ref[2 * p + 1] = jnp.where(lane >= IDX_DIM, slab, zero)

    qpos = i * tq + lax.broadcasted_iota(I32, (1, tq), 1)
    kidx = lax.broadcasted_iota(I32, (ck, tq), 0)

    def score_body(c, carry):
        off = pl.multiple_of(c * ck, ck)
        kd = kd_ref[0, pl.ds(off, ck), :]
        acc = jnp.zeros((ck, tq), F32)
        for h in range(IDX_HEADS):
            x = _dot_nt(kd, qm_ref[h])
            acc = acc + wt_ref[0, h:h + 1, :] * jnp.maximum(x, 0.0)
        key = jnp.where(kidx + off <= qpos, _sortable_key(acc), INT_MIN)
        keys_ref[pl.ds(off, ck), :] = key
        ukey = key ^ INT_MIN
        planes = _bit_transpose32([ukey[r * 8:(r + 1) * 8, :] for r in range(32)])
        for b in range(32):
            planes_ref[c, b] = planes[b]
        cand_ref[c] = jnp.full((8, tq), -1, I32)
        return carry

    lax.fori_loop(0, nk, score_body, 0)

    kk = jnp.minimum(qpos + 1, TOPK)

    def sweep(b, drop, first):
        def body(c, acc):
            cand = cand_ref[c]
            if not first:
                cand = cand & (planes_ref[c, b - 1] ^ drop)
                cand_ref[c] = cand
            return acc + lax.population_count(cand & planes_ref[c, b])
        acc = lax.fori_loop(0, nk, body, jnp.zeros((8, tq), I32))
        return acc.sum(axis=0, keepdims=True)

    def decide(b, n_ones, rem, t_u):
        take = n_ones >= rem
        t_u = jnp.where(take, t_u | jnp.left_shift(1, 31 - b), t_u)
        rem = jnp.where(take, rem, rem - n_ones)
        return rem, t_u, jnp.where(take, 0, -1)

    rem, t_u, drop = decide(0, sweep(0, None, True), kk, jnp.zeros((1, tq), I32))

    def bit_body(b, carry):
        rem, t_u, drop = carry
        return decide(b, sweep(b, drop, False), rem, t_u)

    rem, t_u, drop = lax.fori_loop(1, 32, bit_body, (rem, t_u, drop))

    def last_body(c, acc):
        return acc + lax.population_count(cand_ref[c] & (planes_ref[c, 31] ^ drop))

    n_eq = lax.fori_loop(0, nk, last_body, jnp.zeros((8, tq), I32)).sum(axis=0, keepdims=True)
    t = t_u ^ INT_MIN
    need = rem

    def count(pred):
        def body(c, acc):
            off = pl.multiple_of(c * ck, ck)
            m = jnp.where(pred(keys_ref[pl.ds(off, ck), :], off), 1, 0)
            return acc + m.reshape(ck // 8, 8, tq).sum(axis=0)
        acc = lax.fori_loop(0, nk, body, jnp.zeros((8, tq), I32))
        return acc.sum(axis=0, keepdims=True)

    tie = jnp.max(n_eq - need) > 0
    jbits = 13

    def j_body(it, jcut):
        j_try = jcut + jnp.left_shift(1, jbits - 1 - it)
        n = count(lambda kc, off: (kc == t) & (kidx + off < j_try))
        return jnp.where(n <= need, j_try, jcut)

    jcut = lax.fori_loop(0, jnp.where(tie, jbits, 0), j_body,
                         jnp.where(tie, 0, 2 ** jbits) + jnp.zeros((1, tq), I32))

    def bias_body(c, carry):
        off = pl.multiple_of(c * ck, ck)
        kc = keys_ref[pl.ds(off, ck), :]
        sel = (kc > t) | ((kc == t) & (kidx + off < jcut))
        bias_ref[pl.ds(off, ck), :] = jnp.where(sel, 0.0, NEG_BIG)
        return carry

    lax.fori_loop(0, nk, bias_body, 0)

    ml_ref[0:N_HEADS, :] = jnp.full((N_HEADS, tq), NEG_BIG, F32)
    ml_ref[N_HEADS:, :] = jnp.zeros((N_HEADS, tq), F32)
    acc_ref[...] = jnp.zeros_like(acc_ref)

    def att_body(c, carry):
        for j in range(ck // ATT_SUB):
            off = pl.multiple_of(c * ck + j * ATT_SUB, ATT_SUB)
            bias = bias_ref[pl.ds(off, ATT_SUB), :]
            for h in range(N_HEADS):
                hs = slice(h * HEAD_DIM, (h + 1) * HEAD_DIM)
                s = _dot_nt(k_ref[0, pl.ds(off, ATT_SUB), hs], q_ref[:, hs]) + bias
                m_old = ml_ref[h:h + 1, :]
                m_new = jnp.maximum(m_old, s.max(axis=0, keepdims=True))
                alpha = jnp.exp2(m_old - m_new)
                p = jnp.exp2(s - m_new)
                ml_ref[h:h + 1, :] = m_new
                ml_ref[N_HEADS + h:N_HEADS + h + 1, :] = (
                    ml_ref[N_HEADS + h:N_HEADS + h + 1, :] * alpha + p.sum(axis=0, keepdims=True))
                pv = _dot(vt_ref[0, hs, pl.ds(off, ATT_SUB)], p.astype(BF16))
                acc_ref[h] = acc_ref[h] * alpha + pv
        return carry

    lax.fori_loop(0, nk, att_body, 0)

    for h in range(N_HEADS):
        out = acc_ref[h] / ml_ref[N_HEADS + h:N_HEADS + h + 1, :]
        o_ref[:, h * HEAD_DIM:(h + 1) * HEAD_DIM] = out.T.astype(BF16)


def _attn_prompt(q, qi, wt, k, vt, kd, batch, seq, tq):
    nq = seq // tq
    rows = lambda b, i: (b * nq + i, 0)
    whole = lambda b, i: (b, 0, 0)
    return pl.pallas_call(
        functools.partial(_attn_prompt_kernel, tq=tq),
        grid=(batch, nq),
        in_specs=[pl.BlockSpec((tq, D_ATT), rows),
                  pl.BlockSpec((tq, D_QIDX), rows),
                  pl.BlockSpec((1, IDX_HEADS, tq), lambda b, i: (b, 0, i)),
                  pl.BlockSpec((1, seq, D_ATT), whole),
                  pl.BlockSpec((1, D_ATT, seq), whole),
                  pl.BlockSpec((1, seq, KIWI_W), whole)],
        out_specs=pl.BlockSpec((tq, D_ATT), rows),
        out_shape=jax.ShapeDtypeStruct((batch * seq, D_ATT), BF16),
        scratch_shapes=[pltpu.VMEM((seq, tq), I32),
                        pltpu.VMEM((nq, 32, 8, tq), I32),
                        pltpu.VMEM((nq, 8, tq), I32),
                        pltpu.VMEM((seq, tq), F32),
                        pltpu.VMEM((IDX_HEADS, tq, 128), BF16),
                        pltpu.VMEM((N_HEADS, HEAD_DIM, tq), F32),
                        pltpu.VMEM((2 * N_HEADS, tq), F32)],
        compiler_params=_cparams(("parallel", "arbitrary")),
        name="dsa_prompt",
    )(q, qi, wt, k, vt, kd)


def _select_sample_kernel(pt_ref, qi_ref, w_ref, kn_ref, cache_ref, idx_ref,
                          buf_ref, sem, sc_ref, *, layer, n_pages):
    b = pl.program_id(0)
    past = n_pages * PAGE

    def page_copy(p):
        return pltpu.make_async_copy(cache_ref.at[layer, pt_ref[b, p]], buf_ref.at[p], sem)

    def start(p, c):
        page_copy(p).start()
        return c

    lax.fori_loop(0, n_pages, start, 0)

    def wait(p, c):
        page_copy(p).wait()
        return c

    lax.fori_loop(0, n_pages, wait, 0)

    qi = qi_ref[0]
    w = w_ref[0]
    group = 8

    def score_body(g, c):
        for j in range(group):
            p = g * group + j
            x = _dot(qi, buf_ref[p].astype(BF16))
            sc_ref[pl.ds(p, 1), :] = (w * jnp.maximum(x, 0.0)).sum(axis=0, keepdims=True)
        return c

    lax.fori_loop(0, n_pages // group, score_body, 0)

    kn = kn_ref[0].astype(BF16).astype(F32)
    x_new = (qi.astype(F32) * kn).sum(axis=1, keepdims=True)
    s_new = (w * jnp.maximum(x_new, 0.0)).sum(axis=0, keepdims=True)

    keys = _sortable_key(sc_ref[...])
    key_new = _sortable_key(s_new)
    pos = (lax.broadcasted_iota(I32, (n_pages, PAGE), 0) * PAGE
           + lax.broadcasted_iota(I32, (n_pages, PAGE), 1))

    def count(m_past, m_new):
        return jnp.sum(jnp.where(m_past, 1, 0)) + jnp.sum(jnp.where(m_new, 1, 0))

    def enough(t_try):
        return count(keys >= t_try, key_new >= t_try) >= TOPK

    t = jnp.where(enough(0), 0, INT_MIN)
    t = jnp.where(enough(t | (1 << 30)), t | (1 << 30), t)

    def pair_body(it, t):
        lo = 28 - 2 * it
        t1, t2, t3 = (t | jnp.left_shift(v, lo) for v in (1, 2, 3))
        return jnp.where(enough(t3), t3, jnp.where(enough(t2), t2, jnp.where(enough(t1), t1, t)))

    t = lax.fori_loop(0, 15, pair_body, t)
    need = TOPK - count(keys > t, key_new > t)
    tie = count(keys == t, key_new == t) > need
    jbits = 15

    def j_body(it, jcut):
        j_try = jcut + jnp.left_shift(1, jbits - 1 - it)
        n = count((keys == t) & (pos < j_try), (key_new == t) & (past < j_try))
        return jnp.where(n <= need, j_try, jcut)

    jcut = lax.fori_loop(0, jnp.where(tie, jbits, 0), j_body, jnp.where(tie, 0, 2 ** jbits))
    sel = (keys > t) | ((keys == t) & (pos < jcut))
    n_past = jnp.sum(jnp.where(sel, 1, 0))

    selb = jnp.where(sel, 1.0, 0.0).astype(BF16)
    r_i = lax.broadcasted_iota(I32, (PAGE, PAGE), 0)
    c_i = lax.broadcasted_iota(I32, (PAGE, PAGE), 1)
    incl = _dot(selb, jnp.where(r_i <= c_i, 1.0, 0.0).astype(BF16))
    cnt = jnp.broadcast_to(incl[:, PAGE - 1:PAGE], (n_pages, TOPK))
    pr_i = lax.broadcasted_iota(I32, (n_pages, n_pages), 0)
    pc_i = lax.broadcasted_iota(I32, (n_pages, n_pages), 1)
    pinc = _dot(jnp.where(pc_i <= pr_i, 1.0, 0.0).astype(BF16), cnt.astype(BF16))
    slot = lax.broadcasted_iota(I32, (n_pages, TOPK), 1).astype(F32)
    before = pinc <= slot
    page = jnp.sum(jnp.where(before, 1, 0), axis=0, keepdims=True)
    pexc = jnp.sum(jnp.where(before, cnt, 0.0), axis=0, keepdims=True)
    rank = slot[0:1, :] - pexc
    onehot = jnp.where(lax.broadcasted_iota(I32, (n_pages, TOPK), 0) == page, 1.0, 0.0).astype(BF16)
    incl_sel = _dot(incl.T.astype(BF16), onehot)
    off = jnp.sum(jnp.where(incl_sel <= rank, 1, 0), axis=0, keepdims=True)
    slot_i = lax.broadcasted_iota(I32, (1, TOPK), 1)
    idx_ref[0] = jnp.where(slot_i < n_past, page * PAGE + off, past)


def _select_sample(page_table, qi, w, ki_new, cache_kidx_t, layer):
    bd, n_pages = page_table.shape
    grid_spec = pltpu.PrefetchScalarGridSpec(
        num_scalar_prefetch=1,
        grid=(bd,),
        in_specs=[pl.BlockSpec((1, IDX_HEADS, IDX_DIM), lambda b, pt: (b, 0, 0)),
                  pl.BlockSpec((1, IDX_HEADS, 1), lambda b, pt: (b, 0, 0)),
                  pl.BlockSpec((1, 1, IDX_DIM), lambda b, pt: (b, 0, 0)),
                  pl.BlockSpec(memory_space=pl.ANY)],
        out_specs=pl.BlockSpec((1, 1, TOPK), lambda b, pt: (b, 0, 0)),
        scratch_shapes=[pltpu.VMEM((n_pages, IDX_DIM, PAGE), F32),
                        pltpu.SemaphoreType.DMA(()),
                        pltpu.VMEM((n_pages, PAGE), F32)],
    )
    return pl.pallas_call(
        functools.partial(_select_sample_kernel, layer=layer, n_pages=n_pages),
        grid_spec=grid_spec,
        out_shape=jax.ShapeDtypeStruct((bd, 1, TOPK), I32),
        compiler_params=_cparams(("arbitrary",)),
        name="dsa_sample_select",
    )(page_table, qi, w, ki_new, cache_kidx_t)


def _attn_sample_kernel(pt_ref, idx_ref, q_ref, kn_ref, vn_ref, ck_ref, cv_ref, o_ref,
                        kbuf_ref, vbuf_ref, sems, *, layer, n_pages):
    b = pl.program_id(0)
    past = n_pages * PAGE

    def row_copies(r):
        s = jnp.minimum(idx_ref[b, r], past - 1)
        page = pt_ref[b, s // PAGE]
        off = s % PAGE
        return (pltpu.make_async_copy(ck_ref.at[layer, page, off], kbuf_ref.at[r], sems.at[0]),
                pltpu.make_async_copy(cv_ref.at[layer, page, off], vbuf_ref.at[r], sems.at[1]))

    def start(r, c):
        ck, cv = row_copies(r)
        ck.start()
        cv.start()
        return c

    lax.fori_loop(0, TOPK, start, 0, unroll=8)

    def wait(r, c):
        ck, cv = row_copies(r)
        ck.wait()
        cv.wait()
        return c

    lax.fori_loop(0, TOPK, wait, 0)

    @pl.when(idx_ref[b, TOPK - 1] == past)
    def _():
        kbuf_ref[TOPK - 1] = kn_ref[0]
        vbuf_ref[TOPK - 1] = vn_ref[0]

    q = q_ref[...]
    logits = (kbuf_ref[...] * q).sum(axis=-1, keepdims=True)
    m = logits.max(axis=0, keepdims=True)
    p = jnp.exp2(logits - m)
    den = p.sum(axis=0, keepdims=True)
    o_ref[...] = (p * vbuf_ref[...]).sum(axis=0, keepdims=True) / den


def _attn_sample(page_table, idx, q, k_new, v_new, cache_k, cache_v, layer):
    bd, n_pages = page_table.shape
    row = pl.BlockSpec((1, N_HEADS, HEAD_DIM), lambda b, pt, ix: (b, 0, 0))
    grid_spec = pltpu.PrefetchScalarGridSpec(
        num_scalar_prefetch=2,
        grid=(bd,),
        in_specs=[row, row, row, pl.BlockSpec(memory_space=pl.ANY), pl.BlockSpec(memory_space=pl.ANY)],
        out_specs=row,
        scratch_shapes=[pltpu.VMEM((TOPK, N_HEADS, HEAD_DIM), F32),
                        pltpu.VMEM((TOPK, N_HEADS, HEAD_DIM), F32),
                        pltpu.SemaphoreType.DMA((2,))],
    )
    return pl.pallas_call(
        functools.partial(_attn_sample_kernel, layer=layer, n_pages=n_pages),
        grid_spec=grid_spec,
        out_shape=jax.ShapeDtypeStruct((bd, N_HEADS, HEAD_DIM), F32),
        compiler_params=_cparams(("arbitrary",)),
        name="dsa_sample_attend",
    )(page_table, idx, q, k_new, v_new, cache_k, cache_v)


def _layer_norm(z, g, b):
    mu = jnp.mean(z, axis=-1, keepdims=True)
    zc = z - mu
    var = jnp.mean(zc * zc, axis=-1, keepdims=True)
    return zc * lax.rsqrt(var + LN_EPS) * g + b


def _mix_ln_kernel(yl_ref, ya_ref, w_ref, x_ref, g_ref, b_ref, of_ref, ob_ref, *, alpha):
    for rows in _row_blocks(x_ref.shape[0]):
        mix = _dot(yl_ref[rows, :], w_ref[:D_LRU, :]) + _dot(ya_ref[rows, :], w_ref[D_LRU:, :])
        y = _layer_norm(alpha * x_ref[rows, :] + mix, g_ref[...], b_ref[...])
        of_ref[rows, :] = y
        ob_ref[rows, :] = y.astype(BF16)


def _mix_ln(y_lru, y_att, w_out, x, g, b, layer, tm, alpha):
    m = x.shape[0]
    row = lambda i: (i, 0)
    fixed = lambda i: (0, 0)
    return pl.pallas_call(
        functools.partial(_mix_ln_kernel, alpha=alpha),
        grid=(m // tm,),
        in_specs=[pl.BlockSpec((tm, D_LRU), row), pl.BlockSpec((tm, D_ATT), row),
                  pl.BlockSpec((None, D_LRU + D_ATT, D_MODEL), lambda i: (layer, 0, 0)),
                  pl.BlockSpec((tm, D_MODEL), row),
                  pl.BlockSpec((1, D_MODEL), fixed), pl.BlockSpec((1, D_MODEL), fixed)],
        out_specs=[pl.BlockSpec((tm, D_MODEL), row), pl.BlockSpec((tm, D_MODEL), row)],
        out_shape=[jax.ShapeDtypeStruct((m, D_MODEL), F32), jax.ShapeDtypeStruct((m, D_MODEL), BF16)],
        compiler_params=_cparams(("parallel",)),
        name="mix_ln1",
    )(y_lru, y_att, w_out, x, g, b)


def _ffn_up_kernel(xp_ref, xs_ref, wg_ref, wu_ref, op_ref, os_ref, wgb_ref, wub_ref):
    def act(x):
        g = _dot(x, wgb_ref[...])
        u = _dot(x, wub_ref[...])
        return (g * _sigmoid(g) * u).astype(BF16)

    @pl.when(pl.program_id(1) == 0)
    def _():
        _stage_weight(wg_ref, wgb_ref)
        _stage_weight(wu_ref, wub_ref)
        os_ref[...] = act(xs_ref[...])

    tm = xp_ref.shape[0]
    for r in range(0, tm, ROW_BLOCK):
        op_ref[r:r + ROW_BLOCK, :] = act(xp_ref[r:r + ROW_BLOCK, :])


def _ffn_up(xp, xs, wg, wu, layer, tm, tn):
    mp, k = xp.shape
    ms = xs.shape[0]
    wspec = pl.BlockSpec((None, k, tn), lambda j, i: (layer, 0, j))
    return pl.pallas_call(
        _ffn_up_kernel,
        grid=(D_FF // tn, mp // tm),
        in_specs=[pl.BlockSpec((tm, k), lambda j, i: (i, 0)),
                  pl.BlockSpec((ms, k), lambda j, i: (0, 0)),
                  wspec, wspec],
        out_specs=[pl.BlockSpec((tm, tn), lambda j, i: (i, j)),
                   pl.BlockSpec((ms, tn), lambda j, i: (0, j))],
        out_shape=[jax.ShapeDtypeStruct((mp, D_FF), BF16), jax.ShapeDtypeStruct((ms, D_FF), BF16)],
        scratch_shapes=[pltpu.VMEM((k, tn), BF16), pltpu.VMEM((k, tn), BF16)],
        compiler_params=_cparams(("parallel", "arbitrary")),
        name="ffn_up",
    )(xp, xs, wg, wu)


def _ffn_down_ln_kernel(h_ref, w_ref, x_ref, g_ref, b_ref, of_ref, ob_ref, *, alpha):
    k = pl.program_id(1)
    last = pl.num_programs(1) - 1
    tm, tk = h_ref.shape
    blocks = [slice(r, r + min(tm, ROW_BLOCK)) for r in range(0, tm, ROW_BLOCK)]

    def part(rows):
        return _dot(h_ref[rows, :], w_ref[pl.ds(pl.multiple_of(k * tk, tk), tk), :])

    @pl.when(k == 0)
    def _():
        for rows in blocks:
            of_ref[rows, :] = part(rows)

    @pl.when((k > 0) & (k < last))
    def _():
        for rows in blocks:
            of_ref[rows, :] += part(rows)

    @pl.when(k == last)
    def _():
        for rows in blocks:
            z = alpha * x_ref[rows, :] + (of_ref[rows, :] + part(rows))
            y = _layer_norm(z, g_ref[...], b_ref[...])
            of_ref[rows, :] = y
            ob_ref[rows, :] = y.astype(BF16)


def _ffn_down_ln(h, wd, x, g, b, layer, tm, tk, alpha):
    m = x.shape[0]
    row = lambda i, k: (i, 0)
    fixed = lambda i, k: (0, 0)
    assert D_FF // tk >= 2
    return pl.pallas_call(
        functools.partial(_ffn_down_ln_kernel, alpha=alpha),
        grid=(m // tm, D_FF // tk),
        in_specs=[pl.BlockSpec((tm, tk), lambda i, k: (i, k)),
                  pl.BlockSpec((None, D_FF, D_MODEL), lambda i, k: (layer, 0, 0),
                               pipeline_mode=pl.Buffered(1)),
                  pl.BlockSpec((tm, D_MODEL), row),
                  pl.BlockSpec((1, D_MODEL), fixed), pl.BlockSpec((1, D_MODEL), fixed)],
        out_specs=[pl.BlockSpec((tm, D_MODEL), row), pl.BlockSpec((tm, D_MODEL), row)],
        out_shape=[jax.ShapeDtypeStruct((m, D_MODEL), F32), jax.ShapeDtypeStruct((m, D_MODEL), BF16)],
        compiler_params=_cparams(("parallel", "arbitrary")),
        name="ffn_down_ln2",
    )(h, wd, x, g, b)


def _tiles(rows):
    if rows <= SAMPLE_ROWS:
        return dict(mix=rows, down=rows)
    return dict(proj=1024, mix=512, up=1024, down=512)


def _project(xp16, xs16, w, layer, tabs_p, tabs_s, k_stack, v_stack):
    tm = _tiles(xp16.shape[0])["proj"]
    heads = functools.partial(_proj_heads, xp16, xs16, w["in_t"], layer=layer, tm=tm)
    t128 = (tabs_p[:2], tabs_s[:2])
    t64 = (tabs_p[2:], tabs_s[2:])
    out = {}
    out["xg"] = _proj_lru(xp16, xs16, w["in_t"], layer, tm, 1024)
    out["q"] = heads(*t128, col=O_QKV, half=HEAD_DIM // 2, scale=Q_SCALE, keep_f32=False, name="proj_q")
    kfp, kbp, kfs, kbs = heads(*t128, col=O_QKV + D_ATT, half=HEAD_DIM // 2, scale=1.0,
                               keep_f32=True, name="proj_k", stack=k_stack)
    vfp, vbp, vfs, vbs = heads(*t128, col=O_QKV + 2 * D_ATT, half=0, scale=1.0,
                               keep_f32=True, name="proj_v", stack=v_stack)
    out["kf"], out["kb"], out["vf"], out["vb"] = (kfp, kfs), (kbp, kbs), (vfp, vfs), (vbp, vbs)
    qip, fp, bp_, qis, fs, bs = _proj_indexer(xp16, xs16, w["in_t"], w["kiwi_t"], *t64, layer, tm)
    out["qi"], out["kiwi_f"], out["kiwi_b"] = (qip, qis), (fp, fs), (bp_, bs)
    return out


def _finish(xs, ys_lru, ys_att, w, ln, layer, alpha):
    x1, x1b = [], []
    for x, y_lru, y_att in zip(xs, ys_lru, ys_att):
        a, b = _mix_ln(y_lru, y_att, w["out"], x, ln[0], ln[1], layer, _tiles(x.shape[0])["mix"], alpha)
        x1.append(a)
        x1b.append(b)
    hs = _ffn_up(x1b[0], x1b[1], w["ffn_gate"], w["ffn_up"], layer, _tiles(xs[0].shape[0])["up"], 512)
    return [_ffn_down_ln(h, w["ffn_down"], x, ln[2], ln[3], layer, _tiles(x.shape[0])["down"],
                         D_FF // 2, alpha) for h, x in zip(hs, x1)]


def kernel(x_prompt, x_sample, cache_k, cache_v, cache_kidx, state_lru_h, state_lru_conv, page_table,
           w_in, conv_w, conv_b, gate_a_w, gate_a_b, gate_x_w, gate_x_b, lru_lambda, w_out,
           ln1_g, ln1_b, w_ffn_gate, w_ffn_up, w_ffn_down, ln2_g, ln2_b):
    depth = w_in.shape[0]
    bp, tp, _ = x_prompt.shape
    bd = x_sample.shape[0]
    n_pages = page_table.shape[1]
    past = n_pages * PAGE
    alpha = (2.0 * depth) ** 0.25
    pad = SAMPLE_ROWS - bd

    pos_p = jnp.arange(tp, dtype=I32)
    pos_s = jnp.full((SAMPLE_ROWS,), past, I32)
    tabs_p = _rope_tables(pos_p, HEAD_DIM) + _rope_tables(pos_p, IDX_DIM)
    tabs_s = _rope_tables(pos_s, HEAD_DIM) + _rope_tables(pos_s, IDX_DIM)

    xp = x_prompt.reshape(bp * tp, D_MODEL)
    xs = jnp.pad(x_sample.reshape(bd, D_MODEL), ((0, pad), (0, 0)))
    xp16 = xp.astype(BF16)
    xs16 = xs.astype(BF16)

    w_in_t = jnp.swapaxes(w_in, 1, 2)
    n_kiwi = w_in.shape[2] - O_KIWI
    w = {
        "in_t": w_in_t,
        "kiwi_t": jnp.pad(w_in_t[:, O_KIWI:, :], ((0, 0), (0, KIWI_W - n_kiwi), (0, 0))),
        "out": w_out.astype(BF16),
        "ffn_gate": w_ffn_gate,
        "ffn_up": w_ffn_up,
        "ffn_down": w_ffn_down.astype(BF16),
    }
    gw = jnp.concatenate([gate_a_w, gate_x_w], axis=-1).astype(BF16)
    cache_kidx_t = jnp.swapaxes(cache_kidx, 2, 3)

    outs = {n: [] for n in ("kip", "hp", "cp", "ks", "vs", "kis", "hs", "cs")}
    k_stack = jnp.zeros((depth, bp * tp, D_ATT), F32)
    v_stack = jnp.zeros((depth, bp * tp, D_ATT), F32)
    for l in range(depth):
        ln = (ln1_g[l][None], ln1_b[l][None], ln2_g[l][None], ln2_b[l][None])
        lru_vecs = (conv_w[l], conv_b[l][None])
        gate_vecs = (gate_a_b[l][None], gate_x_b[l][None], lru_lambda[l][None])
        pr = _project(xp16, xs16, w, l, tabs_p, tabs_s, k_stack, v_stack)
        k_stack, v_stack = pr["kf"][0], pr["vf"][0]

        kiwi_f = pr["kiwi_f"][0]
        yp_lru, h_last, conv_last = _lru_prompt(pr["xg"][0], *lru_vecs, gw, *gate_vecs, layer=l,
                                                batch=bp, seq=tp, tc=1024)
        wt = jnp.swapaxes(kiwi_f.reshape(bp, tp, KIWI_W)[:, :, IDX_DIM:IDX_DIM + IDX_HEADS], 1, 2)
        vt = jnp.swapaxes(pr["vb"][0].reshape(bp, tp, D_ATT), 1, 2)
        yp_att = _attn_prompt(pr["q"][0], pr["qi"][0], wt, pr["kb"][0].reshape(bp, tp, D_ATT), vt,
                              pr["kiwi_b"][0].reshape(bp, tp, KIWI_W), batch=bp, seq=tp, tq=256)
        outs["kip"].append(kiwi_f[:, :IDX_DIM].reshape(bp, tp, IDX_DIM))
        outs["hp"].append(h_last)
        outs["cp"].append(conv_last)

        xg, kiwi_f = pr["xg"][1], pr["kiwi_f"][1]
        conv_state = jnp.pad(state_lru_conv[l], ((0, pad), (0, 0), (0, 0)))
        h0 = jnp.pad(state_lru_h[l], ((0, pad), (0, 0)))
        ys_lru, h_new = _lru_sample(xg, conv_state, h0, *lru_vecs, gw[l], *gate_vecs)
        ki_new = kiwi_f[:bd, :IDX_DIM]
        idx = _select_sample(page_table, pr["qi"][1][:bd].reshape(bd, IDX_HEADS, IDX_DIM),
                             kiwi_f[:bd, IDX_DIM:IDX_DIM + IDX_HEADS, None], ki_new[:, None, :],
                             cache_kidx_t, l)
        k_new = pr["kf"][1][:bd].reshape(bd, N_HEADS, HEAD_DIM)
        v_new = pr["vf"][1][:bd].reshape(bd, N_HEADS, HEAD_DIM)
        att = _attn_sample(page_table, idx[:, 0],
                           pr["q"][1][:bd].astype(F32).reshape(bd, N_HEADS, HEAD_DIM),
                           k_new, v_new, cache_k, cache_v, l)
        ys_att = jnp.pad(att.reshape(bd, D_ATT), ((0, pad), (0, 0))).astype(BF16)
        outs["ks"].append(k_new[:, None])
        outs["vs"].append(v_new[:, None])
        outs["kis"].append(ki_new[:, None])
        outs["hs"].append(h_new[:bd])
        outs["cs"].append(jnp.concatenate([state_lru_conv[l][:, 1:], xg[:bd, None, :D_LRU]], axis=1))

        (xp, xp16), (xs, xs16) = _finish((xp, xs), (yp_lru, ys_lru), (yp_att, ys_att), w, ln, l, alpha)

    stack = lambda n: jnp.stack(outs[n])
    return (xp.reshape(bp, tp, D_MODEL), xs[:bd].reshape(bd, 1, D_MODEL),
            k_stack.reshape(depth, bp, tp, N_HEADS, HEAD_DIM),
            v_stack.reshape(depth, bp, tp, N_HEADS, HEAD_DIM), stack("kip"), stack("hp"), stack("cp"),
            stack("ks"), stack("vs"), stack("kis"), stack("hs"), stack("cs"))
```
